```python
import jax, jax.numpy as jnp
from jax import lax
import numpy as np

D_MODEL = 1024
BATCH = 16
SEQ = 256
DEPTH = 2
DEC_BATCH = 4
DEC_SEQ = 2048
PAST_LEN = 256

GRID_W = 64
N_HEADS_A = 8
HEAD_DIM = 64
W_A = N_HEADS_A * HEAD_DIM
W_B = 512
CONV_K = 31
CONV_PAD = CONV_K // 2
WIN_R_MAX = 8
WIN_C = 16
EPS = 1e-6
PROJ_SPLITS = (W_A, 2 * W_A, 3 * W_A, 4 * W_A, 4 * W_A + 2 * W_B, 4 * W_A + 3 * W_B)
D_IN = 4 * W_A + 3 * W_B + 2 * D_MODEL

kernel_name = "gated_natten_conformer_prefix_diffusion"


def rmsnorm(x, g):
    xf = x.astype(jnp.float32)
    y = xf * lax.rsqrt(jnp.mean(xf * xf, axis=-1, keepdims=True) + EPS)
    return (y * g.astype(jnp.float32)).astype(x.dtype)


def layernorm(x, g, b):
    xf = x.astype(jnp.float32)
    mu = jnp.mean(xf, axis=-1, keepdims=True)
    var = jnp.mean(jnp.square(xf - mu), axis=-1, keepdims=True)
    y = (xf - mu) * lax.rsqrt(var + EPS)
    return (y * g.astype(jnp.float32) + b.astype(jnp.float32)).astype(x.dtype)


def context_attention(q, k, v):
    s = jnp.einsum('bqhd,bkhd->bhqk', q, k).astype(jnp.float32) * (HEAD_DIM ** -0.5)
    p = jax.nn.softmax(s, axis=-1).astype(v.dtype)
    o = jnp.einsum('bhqk,bkhd->bqhd', p, v)
    return o.reshape(q.shape[0], q.shape[1], W_A)


def neighbourhood_attention(q, k, v, ck, cv, table):
    B, L, H, Dh = q.shape
    rows = L // GRID_W
    wr = min(WIN_R_MAX, rows)
    r = jnp.arange(rows)
    r0 = jnp.clip(r - wr // 2, 0, rows - wr)
    band = r0[:, None] + jnp.arange(wr)[None, :]
    cq = jnp.arange(GRID_W)
    c0 = jnp.clip(cq - WIN_C // 2, 0, GRID_W - WIN_C)
    col_ok = (cq[None, :] >= c0[:, None]) & (cq[None, :] < c0[:, None] + WIN_C)
    dr = band - r[:, None] + (WIN_R_MAX - 1)
    dc = jnp.clip(cq[None, :] - cq[:, None] + (WIN_C - 1), 0, 2 * WIN_C - 2)
    bias = table[:, dr[:, None, :, None], dc[None, :, None, :]]
    qg = q.reshape(B, rows, GRID_W, H, Dh)
    kb = k.reshape(B, rows, GRID_W, H, Dh)[:, band]
    vb = v.reshape(B, rows, GRID_W, H, Dh)[:, band]
    scale = Dh ** -0.5
    s_win = jnp.einsum('brchd,brkwhd->bhrckw', qg, kb).astype(jnp.float32) * scale
    s_win = s_win + bias.astype(jnp.float32)[None]
    s_win = jnp.where(col_ok[:, None, :], s_win, -jnp.inf)
    s_win = s_win.reshape(B, H, rows, GRID_W, wr * GRID_W)
    s_ctx = jnp.einsum('brchd,bshd->bhrcs', qg, ck).astype(jnp.float32) * scale
    p = jax.nn.softmax(jnp.concatenate([s_win, s_ctx], axis=-1), axis=-1).astype(v.dtype)
    p_win = p[..., :wr * GRID_W].reshape(B, H, rows, GRID_W, wr, GRID_W)
    p_ctx = p[..., wr * GRID_W:]
    o = jnp.einsum('bhrckw,brkwhd->brchd', p_win, vb) + jnp.einsum('bhrcs,bshd->brchd', p_ctx, cv)
    return o.reshape(B, L, H * Dh)


def conv_branch(glu, dw_w, dw_b, ln_g, ln_b):
    a, b = jnp.split(glu, 2, axis=-1)
    u = a * jax.nn.sigmoid(b)
    u = lax.conv_general_dilated(u, dw_w[:, None, :], window_strides=(1,),
                                 padding=[(CONV_PAD, CONV_PAD)],
                                 dimension_numbers=('NWC', 'WIO', 'NWC'),
                                 feature_group_count=W_B) + dw_b
    return jax.nn.silu(layernorm(u, ln_g, ln_b))


def mixer_layer(x, mod, rms_g, w_in, b_in, dw_w, dw_b, ln_g, ln_b, w_proj_a, w_proj_b, w_out, attend):
    shift, scale, gate = jnp.split(mod, 3, axis=-1)
    h = rmsnorm(x, rms_g) * (1 + scale) + shift
    q, k, v, z_a, glu, z_b, gates = jnp.split(h @ w_in + b_in, PROJ_SPLITS, axis=-1)
    B, L, _ = x.shape
    q = q.reshape(B, L, N_HEADS_A, HEAD_DIM)
    k = k.reshape(B, L, N_HEADS_A, HEAD_DIM)
    v = v.reshape(B, L, N_HEADS_A, HEAD_DIM)
    y_a = attend(q, k, v) * jax.nn.silu(z_a)
    y_b = conv_branch(glu, dw_w, dw_b, ln_g, ln_b) * jax.nn.silu(z_b)
    g_a, g_b = jnp.split(jax.nn.sigmoid(gates), 2, axis=-1)
    m = g_a * (y_a @ w_proj_a) + g_b * (y_b @ w_proj_b)
    return x + gate * (m @ w_out), k, v


def setup_inputs(seed: int = 0) -> dict:
    key = jax.random.key(seed)
    ks = jax.random.split(key, 20)
    n = jax.random.normal
    f32 = jnp.float32
    return {
        "x_prompt": n(ks[0], (BATCH, SEQ, D_MODEL), f32),
        "x_sample": n(ks[1], (DEC_BATCH, DEC_SEQ, D_MODEL), f32),
        "cache_k": n(ks[2], (DEC_BATCH, DEPTH, PAST_LEN, N_HEADS_A, HEAD_DIM), f32),
        "cache_v": n(ks[3], (DEC_BATCH, DEPTH, PAST_LEN, N_HEADS_A, HEAD_DIM), f32),
        "c": n(ks[4], (DEC_BATCH, D_MODEL), f32),
        "c_ctx": n(ks[5], (D_MODEL,), f32),
        "rms_g": 1.0 + 0.1 * n(ks[6], (DEPTH, D_MODEL), f32),
        "w_ada": 0.5 * D_MODEL ** -0.5 * n(ks[7], (DEPTH, D_MODEL, 3 * D_MODEL), f32),
        "b_ada": 0.01 * n(ks[8], (DEPTH, 3 * D_MODEL), f32),
        "w_in": D_MODEL ** -0.5 * n(ks[9], (DEPTH, D_MODEL, D_IN), f32),
        "b_in": 0.01 * n(ks[10], (DEPTH, D_IN), f32),
        "rel_bias": 0.1 * n(ks[11], (DEPTH, N_HEADS_A, 2 * WIN_R_MAX - 1, 2 * WIN_C - 1), f32),
        "dw_w": CONV_K ** -0.5 * n(ks[12], (DEPTH, CONV_K, W_B), f32),
        "dw_b": 0.01 * n(ks[13], (DEPTH, W_B), f32),
        "ln_g": 1.0 + 0.1 * n(ks[14], (DEPTH, W_B), f32),
        "ln_b": 0.01 * n(ks[15], (DEPTH, W_B), f32),
        "w_proj_a": W_A ** -0.5 * n(ks[16], (DEPTH, W_A, D_MODEL), f32),
        "w_proj_b": W_B ** -0.5 * n(ks[17], (DEPTH, W_B, D_MODEL), f32),
        "w_out": D_MODEL ** -0.5 * n(ks[18], (DEPTH, D_MODEL, D_MODEL), f32),
        "final_g": 1.0 + 0.1 * n(ks[19], (D_MODEL,), f32),
    }


def reference(x_prompt, x_sample, cache_k, cache_v, c, c_ctx, rms_g, w_ada, b_ada, w_in, b_in,
              rel_bias, dw_w, dw_b, ln_g, ln_b, w_proj_a, w_proj_b, w_out, final_g):
    x = x_prompt
    ks, vs = [], []
    for l in range(DEPTH):
        mod_ctx = jax.nn.silu(c_ctx) @ w_ada[l] + b_ada[l]
        x, k, v = mixer_layer(x, mod_ctx, rms_g[l], w_in[l], b_in[l], dw_w[l], dw_b[l], ln_g[l], ln_b[l],
                              w_proj_a[l], w_proj_b[l], w_out[l], context_attention)
        ks.append(k)
        vs.append(v)
    y_prompt = rmsnorm(x, final_g)
    state_k = jnp.stack(ks, axis=1)
    state_v = jnp.stack(vs, axis=1)

    z = x_sample
    for l in range(DEPTH):
        mod_lat = (jax.nn.silu(c) @ w_ada[l] + b_ada[l])[:, None, :]
        attend = lambda q, k, v, l=l: neighbourhood_attention(q, k, v, cache_k[:, l], cache_v[:, l], rel_bias[l])
        z, _, _ = mixer_layer(z, mod_lat, rms_g[l], w_in[l], b_in[l], dw_w[l], dw_b[l], ln_g[l], ln_b[l],
                              w_proj_a[l], w_proj_b[l], w_out[l], attend)
    y_sample = rmsnorm(z, final_g)
    return (y_prompt, y_sample, state_k, state_v)
```

```python
import functools

import jax
import jax.numpy as jnp
from jax import lax
from jax.experimental import pallas as pl
from jax.experimental.pallas import tpu as pltpu

F32 = jnp.float32
BF16 = jnp.bfloat16

D_MODEL = 1024
N_HEADS = 8
HEAD_DIM = 64
W_A = N_HEADS * HEAD_DIM
W_B = 512
CONV_K = 31
CONV_PAD = CONV_K // 2
GRID_W = 64
WIN_R = 8
WIN_C = 16
EPS = 1e-6
DEPTH = 2

LANES = 128
SUBLANES = 8
MXU_N = 256

ROWS_PER_TILE = 4
TILE = ROWS_PER_TILE * GRID_W
WIN_TILES = 3
HALO = 2 * SUBLANES
HEADS_PER_BLK = MXU_N // HEAD_DIM
N_QKVU = 3 * W_A + 2 * W_B
N_ZG = W_A + W_B + 2 * D_MODEL
MOD_ROWS = 8
MOD_TN = 512
VMEM_LIMIT = 56 * 1024 * 1024
NEG_INF = float("-inf")
N_DR = 2 * WIN_R - 1
N_DC = 2 * WIN_C - 1


def _rmsnorm(x, g):
    return x * lax.rsqrt(jnp.mean(x * x, axis=-1, keepdims=True) + EPS) * g


def _modulated(x, mod, g):
    shift = mod[:, :D_MODEL]
    scale = mod[:, D_MODEL:2 * D_MODEL]
    return _rmsnorm(x, g) * (1.0 + scale) + shift


def _mod_kernel(c_ref, w_ref, b_ref, o_ref):
    s = jax.nn.silu(c_ref[...])
    o_ref[0] = jnp.dot(s.astype(BF16), w_ref[0].astype(BF16), preferred_element_type=F32) + b_ref[0]


def _mods(cstack, w_ada, b_ada):
    n = 3 * D_MODEL
    return pl.pallas_call(
        _mod_kernel,
        grid=(DEPTH, n // MOD_TN),
        in_specs=[
            pl.BlockSpec((MOD_ROWS, D_MODEL), lambda l, j: (0, 0)),
            pl.BlockSpec((1, D_MODEL, MOD_TN), lambda l, j: (l, 0, j)),
            pl.BlockSpec((1, 1, MOD_TN), lambda l, j: (l, 0, j)),
        ],
        out_specs=pl.BlockSpec((1, MOD_ROWS, MOD_TN), lambda l, j: (l, 0, j)),
        out_shape=jax.ShapeDtypeStruct((DEPTH, MOD_ROWS, n), F32),
        name="adaln_mods",
    )(cstack, w_ada, b_ada.reshape(DEPTH, 1, n))


def _bias_kernel(t_ref, e_ref, v_ref):
    h = pl.program_id(0)
    lane = lax.broadcasted_iota(jnp.int32, (GRID_W, LANES), 1)
    cq = lax.broadcasted_iota(jnp.int32, (GRID_W, LANES), 0)
    ck = lane & (GRID_W - 1)
    hi = lane >= GRID_W
    dc = ck - cq + (WIN_C - 1)
    c0 = jnp.clip(cq - WIN_C // 2, 0, GRID_W - WIN_C)
    ok = (ck >= c0) & (ck < c0 + WIN_C)
    n_v = ROWS_PER_TILE * WIN_TILES + ROWS_PER_TILE - 2
    for d in range(n_v):
        acc = jnp.zeros((GRID_W, LANES), F32)
        for j in range(N_DC):
            base = (h * N_DR + d) * N_DC + j
            val = jnp.where(hi, t_ref[base + N_DC], t_ref[base])
            acc = jnp.where(dc == j, val, acc)
        v_ref[d] = jnp.where(ok, acc, NEG_INF)
    for cp in range(WIN_TILES):
        for rq in range(ROWS_PER_TILE):
            for p in range(MXU_N // LANES):
                d = ROWS_PER_TILE * cp + 2 * p - rq + (ROWS_PER_TILE - 1)
                e_ref[0, cp, rq * GRID_W:(rq + 1) * GRID_W, p * LANES:(p + 1) * LANES] = v_ref[d]


def _bias_tiles(table):
    n_v = ROWS_PER_TILE * WIN_TILES + ROWS_PER_TILE - 2
    return pl.pallas_call(
        _bias_kernel,
        grid=(N_HEADS,),
        in_specs=[pl.BlockSpec(memory_space=pltpu.SMEM)],
        out_specs=pl.BlockSpec((1, WIN_TILES, TILE, MXU_N), lambda h: (h, 0, 0, 0)),
        out_shape=jax.ShapeDtypeStruct((N_HEADS, WIN_TILES, TILE, MXU_N), F32),
        scratch_shapes=[pltpu.VMEM((n_v, GRID_W, LANES), F32)],
        name="rel_bias_tiles",
    )(table.reshape(-1))


def _proj_kernel(x_ref, mod_ref, g_ref, w_ref, b_ref, q_ref, k_ref, v_ref, u_ref, *state_refs):
    h = _modulated(x_ref[0], mod_ref[0], g_ref[...])
    p = jnp.dot(h.astype(BF16), w_ref[...], preferred_element_type=F32) + b_ref[...]
    q_ref[0] = (p[:, :W_A] * (HEAD_DIM ** -0.5)).astype(BF16)
    k = p[:, W_A:2 * W_A]
    v = p[:, 2 * W_A:3 * W_A]
    k_ref[0] = k.astype(BF16)
    v_ref[0] = v.astype(BF16)
    a = p[:, 3 * W_A:3 * W_A + W_B]
    gate = p[:, 3 * W_A + W_B:]
    u_ref[0] = a * jax.nn.sigmoid(gate)
    if state_refs:
        state_refs[0][0] = k
        state_refs[1][0] = v


def _proj(x, mods_l, mod_row, g, w, b, with_state):
    bsz, seq, _ = x.shape
    nt = seq // TILE
    tok = lambda i, t: (i, t, 0)
    const2 = lambda i, t: (0, 0)
    out_shape = [jax.ShapeDtypeStruct((bsz, seq, W_A), BF16)] * 3 + [jax.ShapeDtypeStruct((bsz, seq, W_B), F32)]
    out_specs = [pl.BlockSpec((1, TILE, W_A), tok)] * 3 + [pl.BlockSpec((1, TILE, W_B), tok)]
    if with_state:
        out_shape += [jax.ShapeDtypeStruct((bsz, seq, W_A), F32)] * 2
        out_specs += [pl.BlockSpec((1, TILE, W_A), tok)] * 2
    return pl.pallas_call(
        _proj_kernel,
        grid=(bsz, nt),
        in_specs=[
            pl.BlockSpec((1, TILE, D_MODEL), tok),
            pl.BlockSpec((1, 1, 3 * D_MODEL), lambda i, t: (mod_row(i), 0, 0)),
            pl.BlockSpec((1, D_MODEL), const2),
            pl.BlockSpec((D_MODEL, N_QKVU), const2, pipeline_mode=pl.Buffered(1)),
            pl.BlockSpec((1, N_QKVU), const2),
        ],
        out_specs=out_specs,
        out_shape=out_shape,
        compiler_params=pltpu.CompilerParams(
            dimension_semantics=("arbitrary", "arbitrary"), vmem_limit_bytes=VMEM_LIMIT),
        name="proj_qkvu",
    )(x, mods_l, g, w, b)


def _softmax_pv(s_parts, v_parts):
    s = jnp.concatenate(s_parts, axis=1) if len(s_parts) > 1 else s_parts[0]
    m = jnp.max(s, axis=1, keepdims=True)
    p = jnp.exp(s - m)
    l = jnp.sum(p, axis=1, keepdims=True)
    p = p.astype(BF16)
    o = None
    start = 0
    for vp in v_parts:
        n = vp.shape[0]
        part = jnp.dot(p[:, start:start + n], vp, preferred_element_type=F32)
        o = part if o is None else o + part
        start += n
    return o / l


def _mix_kernel(*refs, latent, final, n_tiles):
    if latent:
        (x_ref, mod_ref, g_ref, wz_ref, bz_ref, q_ref, k_ref, v_ref, u_ref, ck_ref, cv_ref, e_ref,
         dww_ref, dwb_ref, lng_ref, lnb_ref, wpa_ref, wpb_ref, wo_ref, fg_ref, y_ref, ubuf) = refs
    else:
        (x_ref, mod_ref, g_ref, wz_ref, bz_ref, q_ref, k_ref, v_ref, u_ref,
         dww_ref, dwb_ref, lng_ref, lnb_ref, wpa_ref, wpb_ref, wo_ref, fg_ref, y_ref, ubuf) = refs
    t = pl.program_id(1)
    x = x_ref[0]
    mod = mod_ref[0]
    h = _modulated(x, mod, g_ref[...])
    zg = jnp.dot(h.astype(BF16), wz_ref[...], preferred_element_type=F32) + bz_ref[...]
    z_a = zg[:, :W_A]
    z_b = zg[:, W_A:W_A + W_B]
    g_a = jax.nn.sigmoid(zg[:, W_A + W_B:W_A + W_B + D_MODEL])
    g_b = jax.nn.sigmoid(zg[:, W_A + W_B + D_MODEL:])

    lane_row = lax.broadcasted_iota(jnp.int32, (1, MXU_N), 1)
    lane_full = lax.broadcasted_iota(jnp.int32, (TILE, MXU_N), 1)
    if latent:
        first_row = ROWS_PER_TILE * t
        win_row = jnp.clip(first_row - ROWS_PER_TILE, 0, (n_tiles - WIN_TILES) * ROWS_PER_TILE)
        win_tok = pl.multiple_of(win_row * GRID_W, TILE)
        tile_off = (win_row - (first_row - ROWS_PER_TILE)) // ROWS_PER_TILE
        n_win = WIN_TILES * TILE
        r = first_row + (lax.broadcasted_iota(jnp.int32, (TILE, n_win), 0) >> 6)
        kr = win_row + (lax.broadcasted_iota(jnp.int32, (TILE, n_win), 1) >> 6)
        r0 = jnp.clip(r - WIN_R // 2, 0, n_tiles * ROWS_PER_TILE - WIN_R)
        band = jnp.where((kr >= r0) & (kr < r0 + WIN_R), 0.0, NEG_INF)
    blocks = []
    for blk in range(N_HEADS // HEADS_PER_BLK):
        cols = slice(blk * MXU_N, (blk + 1) * MXU_N)
        qb = q_ref[0, :, cols]
        if latent:
            kw = k_ref[0, pl.ds(win_tok, n_win), cols]
            vw = v_ref[0, pl.ds(win_tok, n_win), cols]
            ckb = ck_ref[0, :, cols].astype(BF16)
            cvb = cv_ref[0, :, cols].astype(BF16)
        else:
            kw = k_ref[0, :, cols]
            vw = v_ref[0, :, cols]
        acc = jnp.zeros((TILE, MXU_N), F32)
        for hh in range(HEADS_PER_BLK):
            head = blk * HEADS_PER_BLK + hh
            lo = hh * HEAD_DIM
            hm = jnp.where((lane_row >= lo) & (lane_row < lo + HEAD_DIM), 1.0, 0.0).astype(BF16)
            qh = qb * hm
            s = lax.dot_general(qh, kw, (((1,), (1,)), ((), ())), preferred_element_type=F32)
            if latent:
                parts = []
                for i in range(WIN_TILES):
                    e = e_ref[head, jnp.clip(i + tile_off, 0, WIN_TILES - 1)]
                    parts.append(s[:, i * MXU_N:(i + 1) * MXU_N] + e + band[:, i * MXU_N:(i + 1) * MXU_N])
                parts.append(lax.dot_general(qh, ckb, (((1,), (1,)), ((), ())), preferred_element_type=F32))
                o = _softmax_pv(parts, [vw, cvb])
            else:
                o = _softmax_pv([s], [vw])
            acc = jnp.where((lane_full >= lo) & (lane_full < lo + HEAD_DIM), o, acc)
        blocks.append(acc)
    attn = jnp.concatenate(blocks, axis=1)
    y_a = (attn * jax.nn.silu(z_a)).astype(BF16)

    if n_tiles == 1:
        ubuf[0:HALO, :] = jnp.zeros((HALO, W_B), F32)
        ubuf[HALO:HALO + TILE, :] = u_ref[0]
        ubuf[HALO + TILE:, :] = jnp.zeros((HALO, W_B), F32)
    else:
        t0 = pl.multiple_of(t * TILE, TILE)
        left = u_ref[0, pl.ds(pl.multiple_of(jnp.maximum(t0 - HALO, 0), HALO), HALO), :]
        right = u_ref[0, pl.ds(pl.multiple_of(jnp.minimum(t0 + TILE, (n_tiles - 1) * TILE + TILE - HALO), HALO), HALO), :]
        ubuf[0:HALO, :] = jnp.where(t > 0, left, 0.0)
        ubuf[HALO:HALO + TILE, :] = u_ref[0, pl.ds(t0, TILE), :]
        ubuf[HALO + TILE:, :] = jnp.where(t < n_tiles - 1, right, 0.0)
    half = TILE // 2
    span = half + 3 * SUBLANES
    conv_cols = []
    for c in range(W_B // LANES):
        lanes = slice(c * LANES, (c + 1) * LANES)
        halves = []
        for hf in range(2):
            acc = jnp.zeros((half, LANES), F32)
            for rr in range(SUBLANES):
                sh = ubuf[pl.ds(hf * half + rr, span), lanes]
                for j in range(4):
                    kk = SUBLANES * j + rr - (HALO - CONV_PAD)
                    if 0 <= kk < CONV_K:
                        acc = acc + dww_ref[kk:kk + 1, lanes] * sh[SUBLANES * j:SUBLANES * j + half]
            halves.append(acc)
        conv_cols.append(jnp.concatenate(halves, axis=0))
    cv = jnp.concatenate(conv_cols, axis=1) + dwb_ref[...]
    mu = jnp.mean(cv, axis=-1, keepdims=True)
    var = jnp.mean(jnp.square(cv - mu), axis=-1, keepdims=True)
    ln = (cv - mu) * lax.rsqrt(var + EPS) * lng_ref[...] + lnb_ref[...]
    y_b = (jax.nn.silu(ln) * jax.nn.silu(z_b)).astype(BF16)

    m = (g_a * jnp.dot(y_a, wpa_ref[...], preferred_element_type=F32)
         + g_b * jnp.dot(y_b, wpb_ref[...], preferred_element_type=F32))
    o = jnp.dot(m.astype(BF16), wo_ref[...], preferred_element_type=F32)
    y = x + mod[:, 2 * D_MODEL:] * o
    if final:
        y = _rmsnorm(y, fg_ref[...])
    y_ref[0] = y


def _mix(x, mods_l, mod_row, g, wz, bz, q, k, v, u, ctx_kv, e_tiles, layer, dww, dwb, lng, lnb, wpa, wpb, wo, fg, final):
    bsz, seq, _ = x.shape
    nt = seq // TILE
    latent = ctx_kv is not None
    tok = lambda i, t: (i, t, 0)
    seq_all = lambda i, t: (i, 0, 0)
    const2 = lambda i, t: (0, 0)
    single = pl.Buffered(1)
    in_specs = [
        pl.BlockSpec((1, TILE, D_MODEL), tok),
        pl.BlockSpec((1, 1, 3 * D_MODEL), lambda i, t: (mod_row(i), 0, 0)),
        pl.BlockSpec((1, D_MODEL), const2),
        pl.BlockSpec((D_MODEL, N_ZG), const2, pipeline_mode=single),
        pl.BlockSpec((1, N_ZG), const2),
        pl.BlockSpec((1, TILE, W_A), tok),
        pl.BlockSpec((1, seq, W_A), seq_all),
        pl.BlockSpec((1, seq, W_A), seq_all),
        pl.BlockSpec((1, seq, W_B), seq_all),
    ]
    args = [x, mods_l, g, wz, bz, q, k, v, u]
    if latent:
        ck, cv = ctx_kv
        past = ck.shape[2]
        ctx_spec = pl.BlockSpec((1, None, past, W_A), lambda i, t: (i, layer, 0, 0))
        in_specs += [ctx_spec, ctx_spec,
                     pl.BlockSpec((N_HEADS, WIN_TILES, TILE, MXU_N), lambda i, t: (0, 0, 0, 0), pipeline_mode=single)]
        args += [ck, cv, e_tiles]
    in_specs += [
        pl.BlockSpec((CONV_K, W_B), const2),
        pl.BlockSpec((1, W_B), const2),
        pl.BlockSpec((1, W_B), const2),
        pl.BlockSpec((1, W_B), const2),
        pl.BlockSpec((W_A, D_MODEL), const2, pipeline_mode=single),
        pl.BlockSpec((W_B, D_MODEL), const2, pipeline_mode=single),
        pl.BlockSpec((D_MODEL, D_MODEL), const2, pipeline_mode=single),
        pl.BlockSpec((1, D_MODEL), const2),
    ]
    args += [dww, dwb, lng, lnb, wpa, wpb, wo, fg]
    return pl.pallas_call(
        functools.partial(_mix_kernel, latent=latent, final=final, n_tiles=nt),
        grid=(bsz, nt),
        in_specs=in_specs,
        out_specs=pl.BlockSpec((1, TILE, D_MODEL), tok),
        out_shape=jax.ShapeDtypeStruct((bsz, seq, D_MODEL), F32),
        scratch_shapes=[pltpu.VMEM((TILE + 2 * HALO, W_B), F32)],
        compiler_params=pltpu.CompilerParams(
            dimension_semantics=("arbitrary", "arbitrary"), vmem_limit_bytes=VMEM_LIMIT),
        name="mix_latent" if latent else "mix_context",
    )(*args)


def kernel(x_prompt, x_sample, cache_k, cache_v, c, c_ctx, rms_g, w_ada, b_ada, w_in, b_in, rel_bias,
           dw_w, dw_b, ln_g, ln_b, w_proj_a, w_proj_b, w_out, final_g):
    dec_batch = x_sample.shape[0]
    ctx_row = dec_batch
    cstack = jnp.concatenate([c, c_ctx[None, :], jnp.zeros((MOD_ROWS - dec_batch - 1, D_MODEL), F32)], axis=0)
    mods = _mods(cstack, w_ada, b_ada).reshape(DEPTH, MOD_ROWS, 1, 3 * D_MODEL)

    n_att = 3 * W_A
    z_lo, z_hi = n_att, n_att + W_A
    glu_hi = z_hi + 2 * W_B
    w_p = [jnp.concatenate([w_in[l, :, :n_att], w_in[l, :, z_hi:glu_hi]], axis=1).astype(BF16) for l in range(DEPTH)]
    b_p = [jnp.concatenate([b_in[l, :n_att], b_in[l, z_hi:glu_hi]])[None, :] for l in range(DEPTH)]
    w_z = [jnp.concatenate([w_in[l, :, z_lo:z_hi], w_in[l, :, glu_hi:]], axis=1).astype(BF16) for l in range(DEPTH)]
    b_z = [jnp.concatenate([b_in[l, z_lo:z_hi], b_in[l, glu_hi:]])[None, :] for l in range(DEPTH)]
    wpa = w_proj_a.astype(BF16)
    wpb = w_proj_b.astype(BF16)
    wo = w_out.astype(BF16)
    fg = final_g[None, :]
    past = cache_k.shape[2]
    ck = cache_k.reshape(dec_batch, DEPTH, past, W_A)
    cv = cache_v.reshape(dec_batch, DEPTH, past, W_A)

    def layer_args(l):
        return (dw_w[l], dw_b[l][None, :], ln_g[l][None, :], ln_b[l][None, :], wpa[l], wpb[l], wo[l], fg)

    x = x_prompt
    ks, vs = [], []
    ctx_mod = lambda i: ctx_row
    for l in range(DEPTH):
        g = rms_g[l][None, :]
        q, k, v, u, kf, vf = _proj(x, mods[l], ctx_mod, g, w_p[l], b_p[l], True)
        ks.append(kf)
        vs.append(vf)
        x = _mix(x, mods[l], ctx_mod, g, w_z[l], b_z[l], q, k, v, u, None, None, l, *layer_args(l),
                 final=(l == DEPTH - 1))
    bsz, seq, _ = x_prompt.shape
    state_k = jnp.stack(ks, axis=1).reshape(bsz, DEPTH, seq, N_HEADS, HEAD_DIM)
    state_v = jnp.stack(vs, axis=1).reshape(bsz, DEPTH, seq, N_HEADS, HEAD_DIM)

    z = x_sample
    lat_mod = lambda i: i
    for l in range(DEPTH):
        g = rms_g[l][None, :]
        e_tiles = _bias_tiles(rel_bias[l])
        q, k, v, u = _proj(z, mods[l], lat_mod, g, w_p[l], b_p[l], False)
        z = _mix(z, mods[l], lat_mod, g, w_z[l], b_z[l], q, k, v, u, (ck, cv), e_tiles, l, *layer_args(l),
                 final=(l == DEPTH - 1))
    return (x, z, state_k, state_v)
```

```python
import functools

import jax
import jax.numpy as jnp
from jax import lax
from jax.experimental import pallas as pl
from jax.experimental.pallas import tpu as pltpu

F32 = jnp.float32
BF16 = jnp.bfloat16

D_MODEL = 1024
N_HEADS = 8
HEAD_DIM = 64
W_A = N_HEADS * HEAD_DIM
W_B = 512
CONV_K = 31
CONV_PAD = CONV_K // 2
GRID_W = 64
WIN_R = 8
WIN_C = 16
EPS = 1e-6
DEPTH = 2
N_DR = 2 * WIN_R - 1
N_DC = 2 * WIN_C - 1

COL_ZA = 3 * W_A
COL_GLU = COL_ZA + W_A
COL_ZB = COL_GLU + 2 * W_B
D_IN = COL_ZB + W_B + 2 * D_MODEL
N_QKVU = 3 * W_A + 2 * W_B
N_ZG = W_A + W_B + 2 * D_MODEL

LANES = 128
SUBLANES = 8
MXU_N = 256

ROWS_PER_TILE = 4
TILE = ROWS_PER_TILE * GRID_W
WIN_TILES = 3
HALO = 2 * SUBLANES
HEADS_PER_BLK = MXU_N // HEAD_DIM
N_BIAS_V = ROWS_PER_TILE * WIN_TILES + ROWS_PER_TILE - 2
MOD_ROWS = 8
MOD_TN = 512
CAST_TN = 512
VMEM_LIMIT = 56 * 1024 * 1024
NEG_INF = float("-inf")


def _rmsnorm(x, g):
    return x * lax.rsqrt(jnp.mean(x * x, axis=-1, keepdims=True) + EPS) * g


def _modulated(x, mod, g):
    shift = mod[:, :D_MODEL]
    scale = mod[:, D_MODEL:2 * D_MODEL]
    return _rmsnorm(x, g) * (1.0 + scale) + shift


def _mod_kernel(c_ref, w_ref, b_ref, o_ref):
    s = jax.nn.silu(c_ref[...])
    o_ref[0] = jnp.dot(s.astype(BF16), w_ref[0].astype(BF16), preferred_element_type=F32) + b_ref[0]


def _mods(cstack, w_ada, b_ada):
    n = 3 * D_MODEL
    return pl.pallas_call(
        _mod_kernel,
        grid=(DEPTH, n // MOD_TN),
        in_specs=[
            pl.BlockSpec((MOD_ROWS, D_MODEL), lambda l, j: (0, 0)),
            pl.BlockSpec((1, D_MODEL, MOD_TN), lambda l, j: (l, 0, j)),
            pl.BlockSpec((1, 1, MOD_TN), lambda l, j: (l, 0, j)),
        ],
        out_specs=pl.BlockSpec((1, MOD_ROWS, MOD_TN), lambda l, j: (l, 0, j)),
        out_shape=jax.ShapeDtypeStruct((DEPTH, MOD_ROWS, n), F32),
        name="adaln_mods",
    )(cstack, w_ada, b_ada.reshape(DEPTH, 1, n))


def _cast_kernel(w_ref, o_ref):
    o_ref[...] = w_ref[...].astype(BF16)


def _cast_cols(w, n_out, src_block, name):
    depth, kdim, _ = w.shape
    return pl.pallas_call(
        _cast_kernel,
        grid=(depth, n_out // CAST_TN),
        in_specs=[pl.BlockSpec((1, kdim, CAST_TN), lambda l, j: (l, 0, src_block(j)))],
        out_specs=pl.BlockSpec((1, kdim, CAST_TN), lambda l, j: (l, 0, j)),
        out_shape=jax.ShapeDtypeStruct((depth, kdim, n_out), BF16),
        name=name,
    )(w)


def _bias_kernel(t_ref, e_ref, v_ref, *, layer):
    h = pl.program_id(0)
    lane = lax.broadcasted_iota(jnp.int32, (GRID_W, LANES), 1)
    cq = lax.broadcasted_iota(jnp.int32, (GRID_W, LANES), 0)
    ck = lane & (GRID_W - 1)
    hi = lane >= GRID_W
    dc = ck - cq + (WIN_C - 1)
    c0 = jnp.clip(cq - WIN_C // 2, 0, GRID_W - WIN_C)
    ok = (ck >= c0) & (ck < c0 + WIN_C)
    for d in range(N_BIAS_V):
        acc = jnp.zeros((GRID_W, LANES), F32)
        for j in range(N_DC):
            base = ((layer * N_HEADS + h) * N_DR + d) * N_DC + j
            val = jnp.where(hi, t_ref[base + N_DC], t_ref[base])
            acc = jnp.where(dc == j, val, acc)
        v_ref[d] = jnp.where(ok, acc, NEG_INF)
    for cp in range(WIN_TILES):
        for rq in range(ROWS_PER_TILE):
            for p in range(MXU_N // LANES):
                d = ROWS_PER_TILE * cp + 2 * p - rq + (ROWS_PER_TILE - 1)
                e_ref[0, cp, rq * GRID_W:(rq + 1) * GRID_W, p * LANES:(p + 1) * LANES] = v_ref[d]


def _bias_tiles(table_flat, layer):
    return pl.pallas_call(
        functools.partial(_bias_kernel, layer=layer),
        grid=(N_HEADS,),
        in_specs=[pl.BlockSpec(memory_space=pltpu.SMEM)],
        out_specs=pl.BlockSpec((1, WIN_TILES, TILE, MXU_N), lambda h: (h, 0, 0, 0)),
        out_shape=jax.ShapeDtypeStruct((N_HEADS, WIN_TILES, TILE, MXU_N), F32),
        scratch_shapes=[pltpu.VMEM((N_BIAS_V, GRID_W, LANES), F32)],
        name="rel_bias_tiles",
    )(table_flat)


def _proj_kernel(x_ref, mod_ref, g_ref, w_ref, b_ref, q_ref, k_ref, v_ref, u_ref, *state_refs):
    h = _modulated(x_ref[0], mod_ref[0], g_ref[...])
    bias = jnp.concatenate([b_ref[:, :COL_ZA], b_ref[:, COL_GLU:COL_ZB]], axis=1)
    p = jnp.dot(h.astype(BF16), w_ref[...], preferred_element_type=F32) + bias
    q_ref[0] = (p[:, :W_A] * (HEAD_DIM ** -0.5)).astype(BF16)
    k = p[:, W_A:2 * W_A]
    v = p[:, 2 * W_A:3 * W_A]
    k_ref[0] = k.astype(BF16)
    v_ref[0] = v.astype(BF16)
    a = p[:, 3 * W_A:3 * W_A + W_B]
    gate = p[:, 3 * W_A + W_B:]
    u_ref[0] = a * jax.nn.sigmoid(gate)
    if state_refs:
        state_refs[0][0] = k
        state_refs[1][0] = v


def _layer_spec(shape, layer):
    return pl.BlockSpec((None,) + tuple(shape), lambda i, t: (layer,) + (0,) * len(shape))


def _proj(x, mods, mod_row, layer, rms_g, w_p, b_in, with_state):
    bsz, seq, _ = x.shape
    nt = seq // TILE
    tok = lambda i, t: (i, t, 0)
    out_shape = [jax.ShapeDtypeStruct((bsz, seq, W_A), BF16)] * 3 + [jax.ShapeDtypeStruct((bsz, seq, W_B), F32)]
    out_specs = [pl.BlockSpec((1, TILE, W_A), tok)] * 3 + [pl.BlockSpec((1, TILE, W_B), tok)]
    if with_state:
        out_shape += [jax.ShapeDtypeStruct((bsz, seq, W_A), F32)] * 2
        out_specs += [pl.BlockSpec((1, TILE, W_A), tok)] * 2
    return pl.pallas_call(
        _proj_kernel,
        grid=(bsz, nt),
        in_specs=[
            pl.BlockSpec((1, TILE, D_MODEL), tok),
            pl.BlockSpec((None, 1, 1, 3 * D_MODEL), lambda i, t: (layer, mod_row(i), 0, 0)),
            _layer_spec((1, D_MODEL), layer),
            pl.BlockSpec((None, D_MODEL, N_QKVU), lambda i, t: (layer, 0, 0), pipeline_mode=pl.Buffered(1)),
            _layer_spec((1, D_IN), layer),
        ],
        out_specs=out_specs,
        out_shape=out_shape,
        compiler_params=pltpu.CompilerParams(
            dimension_semantics=("arbitrary", "arbitrary"), vmem_limit_bytes=VMEM_LIMIT),
        name="proj_qkvu",
    )(x, mods, rms_g, w_p, b_in)


def _softmax_pv(s_parts, v_parts):
    s = jnp.concatenate(s_parts, axis=1) if len(s_parts) > 1 else s_parts[0]
    m = jnp.max(s, axis=1, keepdims=True)
    p = jnp.exp(s - m)
    l = jnp.sum(p, axis=1, keepdims=True)
    p = p.astype(BF16)
    o = None
    start = 0
    for vp in v_parts:
        n = vp.shape[0]
        part = jnp.dot(p[:, start:start + n], vp, preferred_element_type=F32)
        o = part if o is None else o + part
        start += n
    return o / l


def _mix_kernel(*refs, latent, final, n_tiles):
    if latent:
        (x_ref, mod_ref, g_ref, wz_ref, b_ref, q_ref, k_ref, v_ref, u_ref, ck_ref, cv_ref, e_ref,
         dww_ref, dwb_ref, lng_ref, lnb_ref, wpa_ref, wpb_ref, wo_ref, fg_ref, y_ref, ubuf) = refs
    else:
        (x_ref, mod_ref, g_ref, wz_ref, b_ref, q_ref, k_ref, v_ref, u_ref,
         dww_ref, dwb_ref, lng_ref, lnb_ref, wpa_ref, wpb_ref, wo_ref, fg_ref, y_ref, ubuf) = refs
    t = pl.program_id(1)
    x = x_ref[0]
    mod = mod_ref[0]
    h = _modulated(x, mod, g_ref[...])
    bias = jnp.concatenate([b_ref[:, COL_ZA:COL_GLU], b_ref[:, COL_ZB:]], axis=1)
    zg = jnp.dot(h.astype(BF16), wz_ref[...], preferred_element_type=F32) + bias
    z_a = zg[:, :W_A]
    z_b = zg[:, W_A:W_A + W_B]
    g_a = jax.nn.sigmoid(zg[:, W_A + W_B:W_A + W_B + D_MODEL])
    g_b = jax.nn.sigmoid(zg[:, W_A + W_B + D_MODEL:])

    lane_row = lax.broadcasted_iota(jnp.int32, (1, MXU_N), 1)
    lane_full = lax.broadcasted_iota(jnp.int32, (TILE, MXU_N), 1)
    if latent:
        first_row = ROWS_PER_TILE * t
        win_row = jnp.clip(first_row - ROWS_PER_TILE, 0, (n_tiles - WIN_TILES) * ROWS_PER_TILE)
        win_tok = pl.multiple_of(win_row * GRID_W, TILE)
        tile_off = (win_row - (first_row - ROWS_PER_TILE)) // ROWS_PER_TILE
        n_win = WIN_TILES * TILE
        r = first_row + (lax.broadcasted_iota(jnp.int32, (TILE, n_win), 0) >> 6)
        kr = win_row + (lax.broadcasted_iota(jnp.int32, (TILE, n_win), 1) >> 6)
        r0 = jnp.clip(r - WIN_R // 2, 0, n_tiles * ROWS_PER_TILE - WIN_R)
        band = jnp.where((kr >= r0) & (kr < r0 + WIN_R), 0.0, NEG_INF)
    blocks = []
    for blk in range(N_HEADS // HEADS_PER_BLK):
        cols = slice(blk * MXU_N, (blk + 1) * MXU_N)
        qb = q_ref[0, :, cols]
        if latent:
            kw = k_ref[0, pl.ds(win_tok, n_win), cols]
            vw = v_ref[0, pl.ds(win_tok, n_win), cols]
            ckb = ck_ref[0, :, cols].astype(BF16)
            cvb = cv_ref[0, :, cols].astype(BF16)
        else:
            kw = k_ref[0, :, cols]
            vw = v_ref[0, :, cols]
        acc = jnp.zeros((TILE, MXU_N), F32)
        for hh in range(HEADS_PER_BLK):
            head = blk * HEADS_PER_BLK + hh
            lo = hh * HEAD_DIM
            hm = jnp.where((lane_row >= lo) & (lane_row < lo + HEAD_DIM), 1.0, 0.0).astype(BF16)
            qh = qb * hm
            s = lax.dot_general(qh, kw, (((1,), (1,)), ((), ())), preferred_element_type=F32)
            if latent:
                parts = []
                for i in range(WIN_TILES):
                    e = e_ref[head, jnp.clip(i + tile_off, 0, WIN_TILES - 1)]
                    parts.append(s[:, i * MXU_N:(i + 1) * MXU_N] + e + band[:, i * MXU_N:(i + 1) * MXU_N])
                parts.append(lax.dot_general(qh, ckb, (((1,), (1,)), ((), ())), preferred_element_type=F32))
                o = _softmax_pv(parts, [vw, cvb])
            else:
                o = _softmax_pv([s], [vw])
            acc = jnp.where((lane_full >= lo) & (lane_full < lo + HEAD_DIM), o, acc)
        blocks.append(acc)
    attn = jnp.concatenate(blocks, axis=1)
    y_a = (attn * jax.nn.silu(z_a)).astype(BF16)

    if n_tiles == 1:
        ubuf[0:HALO, :] = jnp.zeros((HALO, W_B), F32)
        ubuf[HALO:HALO + TILE, :] = u_ref[0]
        ubuf[HALO + TILE:, :] = jnp.zeros((HALO, W_B), F32)
    else:
        t0 = pl.multiple_of(t * TILE, TILE)
        left = u_ref[0, pl.ds(pl.multiple_of(jnp.maximum(t0 - HALO, 0), HALO), HALO), :]
        right = u_ref[0, pl.ds(pl.multiple_of(jnp.minimum(t0 + TILE, n_tiles * TILE - HALO), HALO), HALO), :]
        ubuf[0:HALO, :] = jnp.where(t > 0, left, 0.0)
        ubuf[HALO:HALO + TILE, :] = u_ref[0, pl.ds(t0, TILE), :]
        ubuf[HALO + TILE:, :] = jnp.where(t < n_tiles - 1, right, 0.0)
    half = TILE // 2
    span = half + 4 * SUBLANES
    conv_cols = []
    for c in range(W_B // LANES):
        lanes = slice(c * LANES, (c + 1) * LANES)
        halves = []
        for hf in range(2):
            acc = jnp.zeros((half, LANES), F32)
            base = ubuf[hf * half:hf * half + span, lanes]
            for rr in range(SUBLANES):
                sh = base if rr == 0 else pltpu.roll(base, span - rr, axis=0)
                for j in range(4):
                    kk = SUBLANES * j + rr - (HALO - CONV_PAD)
                    if 0 <= kk < CONV_K:
                        acc = acc + dww_ref[kk:kk + 1, lanes] * sh[SUBLANES * j:SUBLANES * j + half]
            halves.append(acc)
        conv_cols.append(jnp.concatenate(halves, axis=0))
    cv = jnp.concatenate(conv_cols, axis=1) + dwb_ref[...]
    mu = jnp.mean(cv, axis=-1, keepdims=True)
    var = jnp.mean(jnp.square(cv - mu), axis=-1, keepdims=True)
    ln = (cv - mu) * lax.rsqrt(var + EPS) * lng_ref[...] + lnb_ref[...]
    y_b = (jax.nn.silu(ln) * jax.nn.silu(z_b)).astype(BF16)

    m = (g_a * jnp.dot(y_a, wpa_ref[...], preferred_element_type=F32)
         + g_b * jnp.dot(y_b, wpb_ref[...], preferred_element_type=F32))
    o = jnp.dot(m.astype(BF16), wo_ref[...], preferred_element_type=F32)
    y = x + mod[:, 2 * D_MODEL:] * o
    if final:
        y = _rmsnorm(y, fg_ref[...])
    y_ref[0] = y


def _mix(x, mods, mod_row, layer, params, q, k, v, u, ctx_kv, e_tiles, final):
    rms_g, w_z, b_in, dw_w, dw_b, ln_g, ln_b, wpa, wpb, wo, fg = params
    bsz, seq, _ = x.shape
    nt = seq // TILE
    latent = ctx_kv is not None
    tok = lambda i, t: (i, t, 0)
    seq_all = lambda i, t: (i, 0, 0)
    single = pl.Buffered(1)
    weight = lambda kdim, n: pl.BlockSpec((None, kdim, n), lambda i, t: (layer, 0, 0), pipeline_mode=single)
    in_specs = [
        pl.BlockSpec((1, TILE, D_MODEL), tok),
        pl.BlockSpec((None, 1, 1, 3 * D_MODEL), lambda i, t: (layer, mod_row(i), 0, 0)),
        _layer_spec((1, D_MODEL), layer),
        weight(D_MODEL, N_ZG),
        _layer_spec((1, D_IN), layer),
        pl.BlockSpec((1, TILE, W_A), tok),
        pl.BlockSpec((1, seq, W_A), seq_all),
        pl.BlockSpec((1, seq, W_A), seq_all),
        pl.BlockSpec((1, seq, W_B), seq_all),
    ]
    args = [x, mods, rms_g, w_z, b_in, q, k, v, u]
    if latent:
        ck, cv = ctx_kv
        past = ck.shape[2]
        ctx_spec = pl.BlockSpec((1, None, past, W_A), lambda i, t: (i, layer, 0, 0))
        in_specs += [ctx_spec, ctx_spec,
                     pl.BlockSpec((N_HEADS, WIN_TILES, TILE, MXU_N), lambda i, t: (0, 0, 0, 0), pipeline_mode=single)]
        args += [ck, cv, e_tiles]
    in_specs += [
        _layer_spec((CONV_K, W_B), layer),
        _layer_spec((1, W_B), layer),
        _layer_spec((1, W_B), layer),
        _layer_spec((1, W_B), layer),
        weight(W_A, D_MODEL),
        weight(W_B, D_MODEL),
        weight(D_MODEL, D_MODEL),
        pl.BlockSpec((1, D_MODEL), lambda i, t: (0, 0)),
    ]
    args += [dw_w, dw_b, ln_g, ln_b, wpa, wpb, wo, fg]
    return pl.pallas_call(
        functools.partial(_mix_kernel, latent=latent, final=final, n_tiles=nt),
        grid=(bsz, nt),
        in_specs=in_specs,
        out_specs=pl.BlockSpec((1, TILE, D_MODEL), tok),
        out_shape=jax.ShapeDtypeStruct((bsz, seq, D_MODEL), F32),
        scratch_shapes=[pltpu.VMEM((TILE + 2 * HALO, W_B), F32)],
        compiler_params=pltpu.CompilerParams(
            dimension_semantics=("arbitrary", "arbitrary"), vmem_limit_bytes=VMEM_LIMIT),
        name="mix_latent" if latent else "mix_context",
    )(*args)


def kernel(x_prompt, x_sample, cache_k, cache_v, c, c_ctx, rms_g, w_ada, b_ada, w_in, b_in, rel_bias,
           dw_w, dw_b, ln_g, ln_b, w_proj_a, w_proj_b, w_out, final_g):
    dec_batch = x_sample.shape[0]
    ctx_row = dec_batch
    cstack = jnp.concatenate([c, c_ctx[None, :], jnp.zeros((MOD_ROWS - dec_batch - 1, D_MODEL), F32)], axis=0)
    mods = _mods(cstack, w_ada, b_ada).reshape(DEPTH, MOD_ROWS, 1, 3 * D_MODEL)

    glu_blk = COL_GLU // CAST_TN
    za_blk = COL_ZA // CAST_TN
    zb_blk = COL_ZB // CAST_TN
    w_p = _cast_cols(w_in, N_QKVU, lambda j: jnp.where(j < za_blk, j, j + (glu_blk - za_blk)), "cast_w_qkvu")
    w_z = _cast_cols(w_in, N_ZG, lambda j: jnp.where(j == 0, za_blk, j + (zb_blk - 1)), "cast_w_zg")
    ident = lambda j: j
    wpa = _cast_cols(w_proj_a, D_MODEL, ident, "cast_w_proj_a")
    wpb = _cast_cols(w_proj_b, D_MODEL, ident, "cast_w_proj_b")
    wo = _cast_cols(w_out, D_MODEL, ident, "cast_w_out")

    row = lambda a: a.reshape(DEPTH, 1, a.shape[-1])
    params = (row(rms_g), w_z, row(b_in), dw_w, row(dw_b), row(ln_g), row(ln_b), wpa, wpb, wo, final_g[None, :])
    past = cache_k.shape[2]
    ck = cache_k.reshape(dec_batch, DEPTH, past, W_A)
    cv = cache_v.reshape(dec_batch, DEPTH, past, W_A)
    table_flat = rel_bias.reshape(-1)

    x = x_prompt
    ks, vs = [], []
    ctx_mod = lambda i: ctx_row
    for l in range(DEPTH):
        q, k, v, u, kf, vf = _proj(x, mods, ctx_mod, l, params[0], w_p, params[2], True)
        ks.append(kf)
        vs.append(vf)
        x = _mix(x, mods, ctx_mod, l, params, q, k, v, u, None, None, final=(l == DEPTH - 1))
    bsz, seq, _ = x_prompt.shape
    state_k = jnp.stack(ks, axis=1).reshape(bsz, DEPTH, seq, N_HEADS, HEAD_DIM)
    state_v = jnp.stack(vs, axis=1).reshape(bsz, DEPTH, seq, N_HEADS, HEAD_DIM)

    z = x_sample
    lat_mod = lambda i: i
    for l in range(DEPTH):
        e_tiles = _bias_tiles(table_flat, l)
        q, k, v, u = _proj(z, mods, lat_mod, l, params[0], w_p, params[2], False)
        z = _mix(z, mods, lat_mod, l, params, q, k, v, u, (ck, cv), e_tiles, final=(l == DEPTH - 1))
    return (x, z, state_k, state_v)
```

```python
import functools

import jax
import jax.numpy as jnp
from jax import lax
from jax.experimental import pallas as pl
from jax.experimental.pallas import tpu as pltpu

F32 = jnp.float32
BF16 = jnp.bfloat16

D_MODEL = 1024
N_HEADS = 8
HEAD_DIM = 64
W_A = N_HEADS * HEAD_DIM
W_B = 512
CONV_K = 31
CONV_PAD = CONV_K // 2
GRID_W = 64
WIN_R = 8
WIN_C = 16
EPS = 1e-6
DEPTH = 2
N_DR = 2 * WIN_R - 1
N_DC = 2 * WIN_C - 1

COL_ZA = 3 * W_A
COL_GLU = COL_ZA + W_A
COL_ZB = COL_GLU + 2 * W_B
D_IN = COL_ZB + W_B + 2 * D_MODEL
N_QKVU = 3 * W_A + 2 * W_B
N_ZG = W_A + W_B + 2 * D_MODEL

LANES = 128
SUBLANES = 8
MXU_N = 256

ROWS_PER_TILE = 4
TILE = ROWS_PER_TILE * GRID_W
WIN_TILES = 3
N_WIN = WIN_TILES * TILE
HALO = 2 * SUBLANES
CONV_ROW_CHUNKS = 4
HEADS_PER_BLK = MXU_N // HEAD_DIM
N_BLK = N_HEADS // HEADS_PER_BLK
N_BIAS_V = ROWS_PER_TILE * WIN_TILES + ROWS_PER_TILE - 2
N_CHUNK = W_B // LANES
ZG_CHUNK = N_ZG // N_CHUNK
MOD_ROWS = 8
MOD_TN = 512
VMEM_LIMIT = 56 * 1024 * 1024
NEG_INF = float("-inf")


def _rmsnorm(x, g):
    return x * lax.rsqrt(jnp.mean(x * x, axis=-1, keepdims=True) + EPS) * g


def _modulated(x, mod, g):
    shift = mod[:, :D_MODEL]
    scale = mod[:, D_MODEL:2 * D_MODEL]
    return _rmsnorm(x, g) * (1.0 + scale) + shift


def _dot_t(a, b):
    return lax.dot_general(a, b, (((1,), (1,)), ((), ())), preferred_element_type=F32)


def _mod_kernel(c_ref, w_ref, b_ref, o_ref):
    s = jax.nn.silu(c_ref[...])
    o_ref[0] = jnp.dot(s.astype(BF16), w_ref[0].astype(BF16), preferred_element_type=F32) + b_ref[0]


def _mods(cstack, w_ada, b_ada):
    n = 3 * D_MODEL
    return pl.pallas_call(
        _mod_kernel,
        grid=(DEPTH, n // MOD_TN),
        in_specs=[
            pl.BlockSpec((MOD_ROWS, D_MODEL), lambda l, j: (0, 0)),
            pl.BlockSpec((1, D_MODEL, MOD_TN), lambda l, j: (l, 0, j)),
            pl.BlockSpec((1, 1, MOD_TN), lambda l, j: (l, 0, j)),
        ],
        out_specs=pl.BlockSpec((1, MOD_ROWS, MOD_TN), lambda l, j: (l, 0, j)),
        out_shape=jax.ShapeDtypeStruct((DEPTH, MOD_ROWS, n), F32),
        name="adaln_mods",
    )(cstack, w_ada, b_ada.reshape(DEPTH, 1, n))


def _cast_kernel(w_ref, o_ref):
    o_ref[0] = w_ref[...].astype(BF16)


def _cast_cols(w, n_blocks, tn, group, src_block, name):
    depth, kdim, _ = w.shape
    return pl.pallas_call(
        _cast_kernel,
        grid=(depth, n_blocks),
        in_specs=[pl.BlockSpec((1, kdim, tn), lambda l, j: (l, 0, src_block(j)))],
        out_specs=pl.BlockSpec((1, 1, kdim, tn), lambda l, j: (l, j // group, 0, j % group)),
        out_shape=jax.ShapeDtypeStruct((depth, n_blocks // group, kdim, group * tn), BF16),
        name=name,
    )(w)


def _bias_kernel(t_ref, e_ref, v_ref, *, layer):
    h = pl.program_id(0)
    lane = lax.broadcasted_iota(jnp.int32, (GRID_W, LANES), 1)
    cq = lax.broadcasted_iota(jnp.int32, (GRID_W, LANES), 0)
    ck = lane & (GRID_W - 1)
    hi = lane >= GRID_W
    dc = ck - cq + (WIN_C - 1)
    c0 = jnp.clip(cq - WIN_C // 2, 0, GRID_W - WIN_C)
    ok = (ck >= c0) & (ck < c0 + WIN_C)
    for d in range(N_BIAS_V):
        acc = jnp.zeros((GRID_W, LANES), F32)
        for j in range(N_DC):
            base = ((layer * N_HEADS + h) * N_DR + d) * N_DC + j
            val = jnp.where(hi, t_ref[base + N_DC], t_ref[base])
            acc = jnp.where(dc == j, val, acc)
        v_ref[d] = jnp.where(ok, acc, NEG_INF)
    for cp in range(WIN_TILES):
        for rq in range(ROWS_PER_TILE):
            for p in range(MXU_N // LANES):
                d = ROWS_PER_TILE * cp + 2 * p - rq + (ROWS_PER_TILE - 1)
                e_ref[0, cp, rq * GRID_W:(rq + 1) * GRID_W, p * LANES:(p + 1) * LANES] = v_ref[d]


def _bias_tiles(table_flat, layer):
    return pl.pallas_call(
        functools.partial(_bias_kernel, layer=layer),
        grid=(N_HEADS,),
        in_specs=[pl.BlockSpec(memory_space=pltpu.SMEM)],
        out_specs=pl.BlockSpec((1, WIN_TILES, TILE, MXU_N), lambda h: (h, 0, 0, 0)),
        out_shape=jax.ShapeDtypeStruct((N_HEADS, WIN_TILES, TILE, MXU_N), F32),
        scratch_shapes=[pltpu.VMEM((N_BIAS_V, GRID_W, LANES), F32)],
        name="rel_bias_tiles",
    )(table_flat)


def _proj_kernel(*refs, n_state_in, all_layers):
    x_ref, mod_ref, g_ref, w_ref, b_ref = refs[:5]
    q_ref, k_ref, v_ref, u_ref, *state_refs = refs[5 + n_state_in:]
    h = _modulated(x_ref[0], mod_ref[0], g_ref[...])
    bias = jnp.concatenate([b_ref[:, :COL_ZA], b_ref[:, COL_GLU:COL_ZB]], axis=1)
    p = jnp.dot(h.astype(BF16), w_ref[...], preferred_element_type=F32) + bias
    q = (p[:, :W_A] * (HEAD_DIM ** -0.5)).astype(BF16)
    k = p[:, W_A:2 * W_A]
    v = p[:, 2 * W_A:3 * W_A]
    k16 = k.astype(BF16)
    v16 = v.astype(BF16)
    for blk in range(N_BLK):
        cols = slice(blk * MXU_N, (blk + 1) * MXU_N)
        q_ref[0, blk] = q[:, cols]
        k_ref[0, blk] = k16[:, cols]
        v_ref[0, blk] = v16[:, cols]
    a = p[:, 3 * W_A:3 * W_A + W_B]
    gate = p[:, 3 * W_A + W_B:]
    u_ref[0] = a * jax.nn.sigmoid(gate)
    if state_refs:
        for d in (range(DEPTH) if all_layers else range(1)):
            state_refs[0][0, d] = k
            state_refs[1][0, d] = v


def _layer_spec(shape, layer):
    return pl.BlockSpec((None,) + tuple(shape), lambda i, t: (layer,) + (0,) * len(shape))


def _proj(x, mods, mod_row, layer, rms_g, w_p, b_in, state):
    bsz, seq, _ = x.shape
    nt = seq // TILE
    tok = lambda i, t: (i, t, 0)
    blk_tok = lambda i, t: (i, 0, t, 0)
    out_shape = [jax.ShapeDtypeStruct((bsz, N_BLK, seq, MXU_N), BF16)] * 3 + [jax.ShapeDtypeStruct((bsz, seq, W_B), F32)]
    out_specs = [pl.BlockSpec((1, N_BLK, TILE, MXU_N), blk_tok)] * 3 + [pl.BlockSpec((1, TILE, W_B), tok)]
    in_specs = [
        pl.BlockSpec((1, TILE, D_MODEL), tok),
        pl.BlockSpec((None, 1, 1, 3 * D_MODEL), lambda i, t: (layer, mod_row(i), 0, 0)),
        _layer_spec((1, D_MODEL), layer),
        pl.BlockSpec((None, None, D_MODEL, N_QKVU), lambda i, t: (layer, 0, 0, 0), pipeline_mode=pl.Buffered(1)),
        _layer_spec((1, D_IN), layer),
    ]
    args = [x, mods, rms_g, w_p, b_in]
    aliases = {}
    if state is not None:
        out_shape += [jax.ShapeDtypeStruct((bsz, DEPTH, seq, W_A), F32)] * 2
        if state:
            in_specs += [pl.BlockSpec(memory_space=pl.ANY)] * 2
            aliases = {len(args): 4, len(args) + 1: 5}
            args += list(state)
            out_specs += [pl.BlockSpec((1, 1, TILE, W_A), lambda i, t: (i, layer, t, 0))] * 2
        else:
            out_specs += [pl.BlockSpec((1, DEPTH, TILE, W_A), lambda i, t: (i, 0, t, 0))] * 2
    return pl.pallas_call(
        functools.partial(_proj_kernel, n_state_in=len(state or ()), all_layers=(state == ())),
        grid=(bsz, nt),
        in_specs=in_specs,
        out_specs=out_specs,
        out_shape=out_shape,
        input_output_aliases=aliases,
        compiler_params=pltpu.CompilerParams(
            dimension_semantics=("arbitrary", "arbitrary"), vmem_limit_bytes=VMEM_LIMIT),
        name="proj_qkvu",
    )(*args)


def _mix_kernel(*refs, latent, final, n_tiles):
    if latent:
        (x_ref, mod_ref, g_ref, wz_ref, b_ref, q_ref, k_ref, v_ref, u_ref, ck_ref, cv_ref, e_ref,
         dww_ref, dwb_ref, lng_ref, lnb_ref, wpa_ref, wpb_ref, wo_ref, fg_ref, y_ref,
         h16, zg_s, bias4, ubuf4, dww4, cv4, s_s, p_s, l_s, attn_s, band_s, ck16, cv16) = refs
    else:
        (x_ref, mod_ref, g_ref, wz_ref, b_ref, q_ref, k_ref, v_ref, u_ref,
         dww_ref, dwb_ref, lng_ref, lnb_ref, wpa_ref, wpb_ref, wo_ref, fg_ref, y_ref,
         h16, zg_s, bias4, ubuf4, dww4, cv4, s_s, p_s, l_s, attn_s) = refs
    t = pl.program_id(1)
    mod = mod_ref[0]

    h16[...] = _modulated(x_ref[0], mod, g_ref[...]).astype(BF16)
    bias = jnp.concatenate([b_ref[:, COL_ZA:COL_GLU], b_ref[:, COL_ZB:]], axis=1)
    for j in range(N_CHUNK):
        bias4[j] = bias[:, j * ZG_CHUNK:(j + 1) * ZG_CHUNK]
    if n_tiles == 1:
        left = jnp.zeros((HALO, W_B), F32)
        right = left
        centre = u_ref[0]
    else:
        t0 = pl.multiple_of(t * TILE, TILE)
        left = u_ref[0, pl.ds(pl.multiple_of(jnp.maximum(t0 - HALO, 0), HALO), HALO), :]
        right = u_ref[0, pl.ds(pl.multiple_of(jnp.minimum(t0 + TILE, n_tiles * TILE - HALO), HALO), HALO), :]
        left = jnp.where(t > 0, left, 0.0)
        right = jnp.where(t < n_tiles - 1, right, 0.0)
        centre = u_ref[0, pl.ds(t0, TILE), :]
    for c in range(N_CHUNK):
        lanes = slice(c * LANES, (c + 1) * LANES)
        ubuf4[c, 0:HALO, :] = left[:, lanes]
        ubuf4[c, HALO:HALO + TILE, :] = centre[:, lanes]
        ubuf4[c, HALO + TILE:, :] = right[:, lanes]
        dww4[c, 0:CONV_K, :] = dww_ref[:, lanes]

    half = TILE // CONV_ROW_CHUNKS
    span = half + 4 * SUBLANES

    def zg_conv(j, carry):
        for hf in range(CONV_ROW_CHUNKS):
            acc = jnp.zeros((half, LANES), F32)
            base = ubuf4[j, hf * half:hf * half + span, :]
            for rr in range(SUBLANES):
                sh = base if rr == 0 else pltpu.roll(base, span - rr, axis=0)
                for jj in range(4):
                    kk = SUBLANES * jj + rr - (HALO - CONV_PAD)
                    if 0 <= kk < CONV_K:
                        acc = acc + dww4[j, kk:kk + 1, :] * sh[SUBLANES * jj:SUBLANES * jj + half]
            cv4[j, hf * half:(hf + 1) * half, :] = acc
        zg_s[j] = jnp.dot(h16[...], wz_ref[j], preferred_element_type=F32) + bias4[j]
        return carry

    lax.fori_loop(0, N_CHUNK, zg_conv, 0)

    lane_row = lax.broadcasted_iota(jnp.int32, (1, MXU_N), 1)
    lane_full = lax.broadcasted_iota(jnp.int32, (TILE, MXU_N), 1)
    if latent:
        first_row = ROWS_PER_TILE * t
        win_row = jnp.clip(first_row - ROWS_PER_TILE, 0, (n_tiles - WIN_TILES) * ROWS_PER_TILE)
        win_tok = pl.multiple_of(win_row * GRID_W, TILE)
        tile_off = (win_row - (first_row - ROWS_PER_TILE)) // ROWS_PER_TILE
        r = first_row + (lax.broadcasted_iota(jnp.int32, (TILE, N_WIN), 0) >> 6)
        kr = win_row + (lax.broadcasted_iota(jnp.int32, (TILE, N_WIN), 1) >> 6)
        r0 = jnp.clip(r - WIN_R // 2, 0, n_tiles * ROWS_PER_TILE - WIN_R)
        band_s[...] = jnp.where((kr >= r0) & (kr < r0 + WIN_R), 0.0, NEG_INF)

        @pl.when(t == 0)
        def _():
            for blk in range(N_BLK):
                cols = slice(blk * MXU_N, (blk + 1) * MXU_N)
                ck16[blk] = ck_ref[0, :, cols].astype(BF16)
                cv16[blk] = cv_ref[0, :, cols].astype(BF16)

    def scores(head, slot):
        blk = head // HEADS_PER_BLK
        lo = (head % HEADS_PER_BLK) * HEAD_DIM
        hm = jnp.where((lane_row >= lo) & (lane_row < lo + HEAD_DIM), 1.0, 0.0).astype(BF16)
        qh = q_ref[0, blk] * hm
        if latent:
            s = _dot_t(qh, k_ref[0, blk, pl.ds(win_tok, N_WIN), :])
            for i in range(WIN_TILES):
                cols = slice(i * MXU_N, (i + 1) * MXU_N)
                e = e_ref[head, jnp.clip(i + tile_off, 0, WIN_TILES - 1)]
                s_s[slot, :, cols] = s[:, cols] + e + band_s[:, cols]
            s_s[slot, :, N_WIN:] = _dot_t(qh, ck16[blk])
        else:
            s_s[slot] = _dot_t(qh, k_ref[0, blk])

    def softmax(slot):
        s = s_s[slot]
        p = jnp.exp(s - jnp.max(s, axis=1, keepdims=True))
        l_s[slot] = jnp.broadcast_to(jnp.sum(p, axis=1, keepdims=True), (TILE, LANES))
        p_s[slot] = p.astype(BF16)

    def values(head, slot):
        blk = head // HEADS_PER_BLK
        lo = (head % HEADS_PER_BLK) * HEAD_DIM
        if latent:
            o = (jnp.dot(p_s[slot, :, :N_WIN], v_ref[0, blk, pl.ds(win_tok, N_WIN), :], preferred_element_type=F32)
                 + jnp.dot(p_s[slot, :, N_WIN:], cv16[blk], preferred_element_type=F32))
        else:
            o = jnp.dot(p_s[slot], v_ref[0, blk], preferred_element_type=F32)
        inv = 1.0 / l_s[slot]
        o = o * jnp.concatenate([inv] * (MXU_N // LANES), axis=1)
        keep = (lane_full >= lo) & (lane_full < lo + HEAD_DIM)
        pltpu.store(attn_s.at[blk], o, mask=keep)

    scores(0, 0)
    scores(1, 1)
    softmax(0)

    def attend(pair, carry):
        i = 2 * pair + 1
        softmax(1)
        scores(i + 1, 0)
        values(i - 1, 0)
        softmax(0)
        scores(i + 2, 1)
        values(i, 1)
        return carry

    lax.fori_loop(0, (N_HEADS - 2) // 2, attend, 0)
    softmax(1)
    values(N_HEADS - 2, 0)
    values(N_HEADS - 1, 1)

    cv = jnp.concatenate([cv4[c] for c in range(N_CHUNK)], axis=1) + dwb_ref[...]
    mu = jnp.mean(cv, axis=-1, keepdims=True)
    var = jnp.mean(jnp.square(cv - mu), axis=-1, keepdims=True)
    ln = (cv - mu) * lax.rsqrt(var + EPS) * lng_ref[...] + lnb_ref[...]
    zg = jnp.concatenate([zg_s[j] for j in range(N_CHUNK)], axis=1)
    z_a = zg[:, :W_A]
    z_b = zg[:, W_A:W_A + W_B]
    g_a = jax.nn.sigmoid(zg[:, W_A + W_B:W_A + W_B + D_MODEL])
    g_b = jax.nn.sigmoid(zg[:, W_A + W_B + D_MODEL:])
    attn = jnp.concatenate([attn_s[blk] for blk in range(N_BLK)], axis=1)
    y_a = (attn * jax.nn.silu(z_a)).astype(BF16)
    y_b = (jax.nn.silu(ln) * jax.nn.silu(z_b)).astype(BF16)
    m = (g_a * jnp.dot(y_a, wpa_ref[...], preferred_element_type=F32)
         + g_b * jnp.dot(y_b, wpb_ref[...], preferred_element_type=F32))
    o = jnp.dot(m.astype(BF16), wo_ref[...], preferred_element_type=F32)
    y = x_ref[0] + mod[:, 2 * D_MODEL:] * o
    if final:
        y = _rmsnorm(y, fg_ref[...])
    y_ref[0] = y


def _mix(x, mods, mod_row, layer, params, q, k, v, u, ctx_kv, e_tiles, final):
    rms_g, w_z, b_in, dw_w, dw_b, ln_g, ln_b, wpa, wpb, wo, fg = params
    bsz, seq, _ = x.shape
    nt = seq // TILE
    latent = ctx_kv is not None
    n_keys = N_WIN + ctx_kv[0].shape[2] if latent else seq
    tok = lambda i, t: (i, t, 0)
    single = pl.Buffered(1)
    weight = lambda *shape: pl.BlockSpec((None,) + shape, lambda i, t: (layer,) + (0,) * len(shape),
                                         pipeline_mode=single)
    in_specs = [
        pl.BlockSpec((1, TILE, D_MODEL), tok),
        pl.BlockSpec((None, 1, 1, 3 * D_MODEL), lambda i, t: (layer, mod_row(i), 0, 0)),
        _layer_spec((1, D_MODEL), layer),
        weight(N_CHUNK, D_MODEL, ZG_CHUNK),
        _layer_spec((1, D_IN), layer),
        pl.BlockSpec((1, N_BLK, TILE, MXU_N), lambda i, t: (i, 0, t, 0)),
        pl.BlockSpec((1, N_BLK, seq, MXU_N), lambda i, t: (i, 0, 0, 0)),
        pl.BlockSpec((1, N_BLK, seq, MXU_N), lambda i, t: (i, 0, 0, 0)),
        pl.BlockSpec((1, seq, W_B), lambda i, t: (i, 0, 0)),
    ]
    args = [x, mods, rms_g, w_z, b_in, q, k, v, u]
    scratch = [
        pltpu.VMEM((TILE, D_MODEL), BF16),
        pltpu.VMEM((N_CHUNK, TILE, ZG_CHUNK), F32),
        pltpu.VMEM((N_CHUNK, 1, ZG_CHUNK), F32),
        pltpu.VMEM((N_CHUNK, TILE + 2 * HALO, LANES), F32),
        pltpu.VMEM((N_CHUNK, 4 * SUBLANES, LANES), F32),
        pltpu.VMEM((N_CHUNK, TILE, LANES), F32),
        pltpu.VMEM((2, TILE, n_keys), F32),
        pltpu.VMEM((2, TILE, n_keys), BF16),
        pltpu.VMEM((2, TILE, LANES), F32),
        pltpu.VMEM((N_BLK, TILE, MXU_N), F32),
    ]
    if latent:
        ck, cv = ctx_kv
        past = ck.shape[2]
        ctx_spec = pl.BlockSpec((1, None, past, W_A), lambda i, t: (i, layer, 0, 0))
        in_specs += [ctx_spec, ctx_spec,
                     pl.BlockSpec((N_HEADS, WIN_TILES, TILE, MXU_N), lambda i, t: (0, 0, 0, 0), pipeline_mode=single)]
        args += [ck, cv, e_tiles]
        scratch += [
            pltpu.VMEM((TILE, N_WIN), F32),
            pltpu.VMEM((N_BLK, past, MXU_N), BF16),
            pltpu.VMEM((N_BLK, past, MXU_N), BF16),
        ]
    in_specs += [
        _layer_spec((CONV_K, W_B), layer),
        _layer_spec((1, W_B), layer),
        _layer_spec((1, W_B), layer),
        _layer_spec((1, W_B), layer),
        weight(None, W_A, D_MODEL),
        weight(None, W_B, D_MODEL),
        weight(None, D_MODEL, D_MODEL),
        pl.BlockSpec((1, D_MODEL), lambda i, t: (0, 0)),
    ]
    args += [dw_w, dw_b, ln_g, ln_b, wpa, wpb, wo, fg]
    return pl.pallas_call(
        functools.partial(_mix_kernel, latent=latent, final=final, n_tiles=nt),
        grid=(bsz, nt),
        in_specs=in_specs,
        out_specs=pl.BlockSpec((1, TILE, D_MODEL), tok),
        out_shape=jax.ShapeDtypeStruct((bsz, seq, D_MODEL), F32),
        scratch_shapes=scratch,
        compiler_params=pltpu.CompilerParams(
            dimension_semantics=("arbitrary", "arbitrary"), vmem_limit_bytes=VMEM_LIMIT),
        name="mix_latent" if latent else "mix_context",
    )(*args)


def kernel(x_prompt, x_sample, cache_k, cache_v, c, c_ctx, rms_g, w_ada, b_ada, w_in, b_in, rel_bias,
           dw_w, dw_b, ln_g, ln_b, w_proj_a, w_proj_b, w_out, final_g):
    dec_batch = x_sample.shape[0]
    ctx_row = dec_batch
    cstack = jnp.concatenate([c, c_ctx[None, :], jnp.zeros((MOD_ROWS - dec_batch - 1, D_MODEL), F32)], axis=0)
    mods = _mods(cstack, w_ada, b_ada).reshape(DEPTH, MOD_ROWS, 1, 3 * D_MODEL)

    tn = 512
    za_blk, glu_blk = COL_ZA // tn, COL_GLU // tn
    w_p = _cast_cols(w_in, N_QKVU // tn, tn, N_QKVU // tn,
                     lambda j: jnp.where(j < za_blk, j, j + (glu_blk - za_blk)), "cast_w_qkvu")
    tz = MXU_N
    za_z, zb_z = COL_ZA // tz, COL_ZB // tz
    n_za = W_A // tz
    w_z = _cast_cols(w_in, N_ZG // tz, tz, ZG_CHUNK // tz,
                     lambda j: jnp.where(j < n_za, j + za_z, j + (zb_z - n_za)), "cast_w_zg")
    ident = lambda j: j
    wpa = _cast_cols(w_proj_a, D_MODEL // tn, tn, D_MODEL // tn, ident, "cast_w_proj_a")
    wpb = _cast_cols(w_proj_b, D_MODEL // tn, tn, D_MODEL // tn, ident, "cast_w_proj_b")
    wo = _cast_cols(w_out, D_MODEL // tn, tn, D_MODEL // tn, ident, "cast_w_out")

    row = lambda a: a.reshape(DEPTH, 1, a.shape[-1])
    params = (row(rms_g), w_z, row(b_in), dw_w, row(dw_b), row(ln_g), row(ln_b), wpa, wpb, wo, final_g[None, :])
    past = cache_k.shape[2]
    ck = cache_k.reshape(dec_batch, DEPTH, past, W_A)
    cv = cache_v.reshape(dec_batch, DEPTH, past, W_A)
    table_flat = rel_bias.reshape(-1)

    x = x_prompt
    state = ()
    ctx_mod = lambda i: ctx_row
    for l in range(DEPTH):
        q, k, v, u, *state = _proj(x, mods, ctx_mod, l, params[0], w_p, params[2], tuple(state))
        x = _mix(x, mods, ctx_mod, l, params, q, k, v, u, None, None, final=(l == DEPTH - 1))
    bsz, seq, _ = x_prompt.shape
    state_k = state[0].reshape(bsz, DEPTH, seq, N_HEADS, HEAD_DIM)
    state_v = state[1].reshape(bsz, DEPTH, seq, N_HEADS, HEAD_DIM)

    z = x_sample
    lat_mod = lambda i: i
    for l in range(DEPTH):
        e_tiles = _bias_tiles(table_flat, l)
        q, k, v, u = _proj(z, mods, lat_mod, l, params[0], w_p, params[2], None)
        z = _mix(z, mods, lat_mod, l, params, q, k, v, u, (ck, cv), e_tiles, final=(l == DEPTH - 1))
    return (x, z, state_k, state_v)
```

```python
import functools

import jax
import jax.numpy as jnp
from jax import lax
from jax.experimental import pallas as pl
from jax.experimental.pallas import tpu as pltpu

F32 = jnp.float32
BF16 = jnp.bfloat16

D_MODEL = 1024
N_HEADS = 8
HEAD_DIM = 64
W_A = N_HEADS * HEAD_DIM
W_B = 512
CONV_K = 31
CONV_PAD = CONV_K // 2
GRID_W = 64
WIN_R = 8
WIN_C = 16
EPS = 1e-6
DEPTH = 2
N_DR = 2 * WIN_R - 1
N_DC = 2 * WIN_C - 1

COL_ZA = 3 * W_A
COL_GLU = COL_ZA + W_A
COL_ZB = COL_GLU + 2 * W_B
D_IN = COL_ZB + W_B + 2 * D_MODEL
N_QKVU = 3 * W_A + 2 * W_B
N_ZG = W_A + W_B + 2 * D_MODEL

LANES = 128
SUBLANES = 8
MXU_N = 256

ROWS_PER_TILE = 4
TILE = ROWS_PER_TILE * GRID_W
PROJ_TILE = 512
WIN_TILES = 3
N_WIN = WIN_TILES * TILE
HALO = 2 * SUBLANES
CONV_ROW_CHUNKS = 2
HEADS_PER_BLK = MXU_N // HEAD_DIM
N_BLK = N_HEADS // HEADS_PER_BLK
N_BIAS_V = ROWS_PER_TILE * WIN_TILES + ROWS_PER_TILE - 2
MOD_ROWS = 8
MOD_TN = 512
CAST_TN = 512
VMEM_LIMIT = 56 * 1024 * 1024
NEG_INF = float("-inf")


def _rmsnorm(x, g):
    return x * lax.rsqrt(jnp.mean(x * x, axis=-1, keepdims=True) + EPS) * g


def _modulated(x, mod, g):
    shift = mod[:, :D_MODEL]
    scale = mod[:, D_MODEL:2 * D_MODEL]
    return _rmsnorm(x, g) * (1.0 + scale) + shift


def _dot_t(a, b):
    return lax.dot_general(a, b, (((1,), (1,)), ((), ())), preferred_element_type=F32)


def _mod_kernel(c_ref, w_ref, b_ref, o_ref):
    s = jax.nn.silu(c_ref[...])
    o_ref[0] = jnp.dot(s.astype(BF16), w_ref[0].astype(BF16), preferred_element_type=F32) + b_ref[0]


def _mods(cstack, w_ada, b_ada):
    n = 3 * D_MODEL
    return pl.pallas_call(
        _mod_kernel,
        grid=(DEPTH, n // MOD_TN),
        in_specs=[
            pl.BlockSpec((MOD_ROWS, D_MODEL), lambda l, j: (0, 0)),
            pl.BlockSpec((1, D_MODEL, MOD_TN), lambda l, j: (l, 0, j)),
            pl.BlockSpec((1, 1, MOD_TN), lambda l, j: (l, 0, j)),
        ],
        out_specs=pl.BlockSpec((1, MOD_ROWS, MOD_TN), lambda l, j: (l, 0, j)),
        out_shape=jax.ShapeDtypeStruct((DEPTH, MOD_ROWS, n), F32),
        name="adaln_mods",
    )(cstack, w_ada, b_ada.reshape(DEPTH, 1, n))


def _cast_kernel(w_ref, o_ref):
    o_ref[...] = w_ref[...].astype(BF16)


def _cast_cols(w, n_out, src_block, name):
    depth, kdim, _ = w.shape
    return pl.pallas_call(
        _cast_kernel,
        grid=(depth, n_out // CAST_TN),
        in_specs=[pl.BlockSpec((1, kdim, CAST_TN), lambda l, j: (l, 0, src_block(j)))],
        out_specs=pl.BlockSpec((1, kdim, CAST_TN), lambda l, j: (l, 0, j)),
        out_shape=jax.ShapeDtypeStruct((depth, kdim, n_out), BF16),
        name=name,
    )(w)


def _bias_kernel(t_ref, e_ref, v_ref, *, layer):
    h = pl.program_id(0)
    lane = lax.broadcasted_iota(jnp.int32, (GRID_W, LANES), 1)
    cq = lax.broadcasted_iota(jnp.int32, (GRID_W, LANES), 0)
    ck = lane & (GRID_W - 1)
    hi = lane >= GRID_W
    dc = ck - cq + (WIN_C - 1)
    c0 = jnp.clip(cq - WIN_C // 2, 0, GRID_W - WIN_C)
    ok = (ck >= c0) & (ck < c0 + WIN_C)
    for d in range(N_BIAS_V):
        acc = jnp.zeros((GRID_W, LANES), F32)
        for j in range(N_DC):
            base = ((layer * N_HEADS + h) * N_DR + d) * N_DC + j
            val = jnp.where(hi, t_ref[base + N_DC], t_ref[base])
            acc = jnp.where(dc == j, val, acc)
        v_ref[d] = jnp.where(ok, acc, NEG_INF)
    for cp in range(WIN_TILES):
        for rq in range(ROWS_PER_TILE):
            for p in range(MXU_N // LANES):
                d = ROWS_PER_TILE * cp + 2 * p - rq + (ROWS_PER_TILE - 1)
                e_ref[0, cp, rq * GRID_W:(rq + 1) * GRID_W, p * LANES:(p + 1) * LANES] = v_ref[d]


def _bias_tiles(table_flat, layer):
    return pl.pallas_call(
        functools.partial(_bias_kernel, layer=layer),
        grid=(N_HEADS,),
        in_specs=[pl.BlockSpec(memory_space=pltpu.SMEM)],
        out_specs=pl.BlockSpec((1, WIN_TILES, TILE, MXU_N), lambda h: (h, 0, 0, 0)),
        out_shape=jax.ShapeDtypeStruct((N_HEADS, WIN_TILES, TILE, MXU_N), F32),
        scratch_shapes=[pltpu.VMEM((N_BIAS_V, GRID_W, LANES), F32)],
        name="rel_bias_tiles",
    )(table_flat)


def _proj_kernel(*refs, n_state_in, all_layers):
    x_ref, mod_ref, g_ref, w_ref, b_ref = refs[:5]
    q_ref, k_ref, v_ref, u_ref, *state_refs = refs[5 + n_state_in:]
    h = _modulated(x_ref[0], mod_ref[0], g_ref[...])
    bias = jnp.concatenate([b_ref[:, :COL_ZA], b_ref[:, COL_GLU:COL_ZB]], axis=1)
    p = jnp.dot(h.astype(BF16), w_ref[...], preferred_element_type=F32) + bias
    q = (p[:, :W_A] * (HEAD_DIM ** -0.5)).astype(BF16)
    k = p[:, W_A:2 * W_A]
    v = p[:, 2 * W_A:3 * W_A]
    k16 = k.astype(BF16)
    v16 = v.astype(BF16)
    for blk in range(N_BLK):
        cols = slice(blk * MXU_N, (blk + 1) * MXU_N)
        q_ref[0, blk] = q[:, cols]
        k_ref[0, blk] = k16[:, cols]
        v_ref[0, blk] = v16[:, cols]
    a = p[:, 3 * W_A:3 * W_A + W_B]
    gate = p[:, 3 * W_A + W_B:]
    u_ref[0] = a * jax.nn.sigmoid(gate)
    if state_refs:
        for d in (range(DEPTH) if all_layers else range(1)):
            state_refs[0][0, d] = k
            state_refs[1][0, d] = v


def _layer_spec(shape, layer):
    return pl.BlockSpec((None,) + tuple(shape), lambda i, t: (layer,) + (0,) * len(shape))


def _weight_spec(shape, layer):
    return pl.BlockSpec((None,) + tuple(shape), lambda i, t: (layer,) + (0,) * len(shape),
                        pipeline_mode=pl.Buffered(1))


def _proj(x, mods, mod_row, layer, rms_g, w_p, b_in, state):
    bsz, seq, _ = x.shape
    tile = min(PROJ_TILE, seq)
    tok = lambda i, t: (i, t, 0)
    blk_tok = lambda i, t: (i, 0, t, 0)
    out_shape = [jax.ShapeDtypeStruct((bsz, N_BLK, seq, MXU_N), BF16)] * 3 + [jax.ShapeDtypeStruct((bsz, seq, W_B), F32)]
    out_specs = [pl.BlockSpec((1, N_BLK, tile, MXU_N), blk_tok)] * 3 + [pl.BlockSpec((1, tile, W_B), tok)]
    in_specs = [
        pl.BlockSpec((1, tile, D_MODEL), tok),
        pl.BlockSpec((None, 1, 1, 3 * D_MODEL), lambda i, t: (layer, mod_row(i), 0, 0)),
        _layer_spec((1, D_MODEL), layer),
        _weight_spec((D_MODEL, N_QKVU), layer),
        _layer_spec((1, D_IN), layer),
    ]
    args = [x, mods, rms_g, w_p, b_in]
    aliases = {}
    if state is not None:
        out_shape += [jax.ShapeDtypeStruct((bsz, DEPTH, seq, W_A), F32)] * 2
        if state:
            in_specs += [pl.BlockSpec(memory_space=pl.ANY)] * 2
            aliases = {len(args): 4, len(args) + 1: 5}
            args += list(state)
            out_specs += [pl.BlockSpec((1, 1, tile, W_A), lambda i, t: (i, layer, t, 0))] * 2
        else:
            out_specs += [pl.BlockSpec((1, DEPTH, tile, W_A), lambda i, t: (i, 0, t, 0))] * 2
    return pl.pallas_call(
        functools.partial(_proj_kernel, n_state_in=len(state or ()), all_layers=(state == ())),
        grid=(bsz, seq // tile),
        in_specs=in_specs,
        out_specs=out_specs,
        out_shape=out_shape,
        input_output_aliases=aliases,
        compiler_params=pltpu.CompilerParams(
            dimension_semantics=("arbitrary", "arbitrary"), vmem_limit_bytes=VMEM_LIMIT),
        name="proj_qkvu",
    )(*args)


def _softmax_pv(s_parts, v_parts):
    s = jnp.concatenate(s_parts, axis=1) if len(s_parts) > 1 else s_parts[0]
    m = jnp.max(s, axis=1, keepdims=True)
    p = jnp.exp(s - m)
    l = jnp.sum(p, axis=1, keepdims=True)
    p = p.astype(BF16)
    o = None
    start = 0
    for vp in v_parts:
        n = vp.shape[0]
        part = jnp.dot(p[:, start:start + n], vp, preferred_element_type=F32)
        o = part if o is None else o + part
        start += n
    return o / l


def _conv_ln(u_ref, ubuf, dww_ref, dwb_ref, lng_ref, lnb_ref, t, n_tiles):
    if n_tiles == 1:
        ubuf[0:HALO, :] = jnp.zeros((HALO, W_B), F32)
        ubuf[HALO:HALO + TILE, :] = u_ref[0]
        ubuf[HALO + TILE:, :] = jnp.zeros((HALO, W_B), F32)
    else:
        t0 = pl.multiple_of(t * TILE, TILE)
        left = u_ref[0, pl.ds(pl.multiple_of(jnp.maximum(t0 - HALO, 0), HALO), HALO), :]
        right = u_ref[0, pl.ds(pl.multiple_of(jnp.minimum(t0 + TILE, n_tiles * TILE - HALO), HALO), HALO), :]
        ubuf[0:HALO, :] = jnp.where(t > 0, left, 0.0)
        ubuf[HALO:HALO + TILE, :] = u_ref[0, pl.ds(t0, TILE), :]
        ubuf[HALO + TILE:, :] = jnp.where(t < n_tiles - 1, right, 0.0)
    half = TILE // CONV_ROW_CHUNKS
    span = half + 4 * SUBLANES
    conv_cols = []
    for c in range(W_B // LANES):
        lanes = slice(c * LANES, (c + 1) * LANES)
        halves = []
        for hf in range(CONV_ROW_CHUNKS):
            acc = jnp.zeros((half, LANES), F32)
            base = ubuf[hf * half:hf * half + span, lanes]
            for rr in range(SUBLANES):
                sh = base if rr == 0 else pltpu.roll(base, span - rr, axis=0)
                for j in range(4):
                    kk = SUBLANES * j + rr - (HALO - CONV_PAD)
                    if 0 <= kk < CONV_K:
                        acc = acc + dww_ref[kk:kk + 1, lanes] * sh[SUBLANES * j:SUBLANES * j + half]
            halves.append(acc)
        conv_cols.append(jnp.concatenate(halves, axis=0))
    cv = jnp.concatenate(conv_cols, axis=1) + dwb_ref[...]
    mu = jnp.mean(cv, axis=-1, keepdims=True)
    var = jnp.mean(jnp.square(cv - mu), axis=-1, keepdims=True)
    return (cv - mu) * lax.rsqrt(var + EPS) * lng_ref[...] + lnb_ref[...]


def _mix_kernel(*refs, latent, final, n_tiles):
    if latent:
        (x_ref, mod_ref, g_ref, wz_ref, b_ref, q_ref, k_ref, v_ref, u_ref, ck_ref, cv_ref, e_ref,
         dww_ref, dwb_ref, lng_ref, lnb_ref, wpa_ref, wpb_ref, wo_ref, fg_ref, y_ref, ubuf) = refs
    else:
        (x_ref, mod_ref, g_ref, wz_ref, b_ref, q_ref, k_ref, v_ref, u_ref,
         dww_ref, dwb_ref, lng_ref, lnb_ref, wpa_ref, wpb_ref, wo_ref, fg_ref, y_ref, ubuf) = refs
    t = pl.program_id(1)
    x = x_ref[0]
    mod = mod_ref[0]
    h = _modulated(x, mod, g_ref[...])
    bias = jnp.concatenate([b_ref[:, COL_ZA:COL_GLU], b_ref[:, COL_ZB:]], axis=1)
    zg = jnp.dot(h.astype(BF16), wz_ref[...], preferred_element_type=F32) + bias
    z_a = zg[:, :W_A]
    z_b = zg[:, W_A:W_A + W_B]
    g_a = jax.nn.sigmoid(zg[:, W_A + W_B:W_A + W_B + D_MODEL])
    g_b = jax.nn.sigmoid(zg[:, W_A + W_B + D_MODEL:])

    lane_row = lax.broadcasted_iota(jnp.int32, (1, MXU_N), 1)
    lane_full = lax.broadcasted_iota(jnp.int32, (TILE, MXU_N), 1)
    if latent:
        first_row = ROWS_PER_TILE * t
        win_row = jnp.clip(first_row - ROWS_PER_TILE, 0, (n_tiles - WIN_TILES) * ROWS_PER_TILE)
        win_tok = pl.multiple_of(win_row * GRID_W, TILE)
        tile_off = (win_row - (first_row - ROWS_PER_TILE)) // ROWS_PER_TILE
        r = first_row + (lax.broadcasted_iota(jnp.int32, (TILE, N_WIN), 0) >> 6)
        kr = win_row + (lax.broadcasted_iota(jnp.int32, (TILE, N_WIN), 1) >> 6)
        r0 = jnp.clip(r - WIN_R // 2, 0, n_tiles * ROWS_PER_TILE - WIN_R)
        band = jnp.where((kr >= r0) & (kr < r0 + WIN_R), 0.0, NEG_INF)
    blocks = []
    for blk in range(N_BLK):
        cols = slice(blk * MXU_N, (blk + 1) * MXU_N)
        qb = q_ref[0, blk]
        if latent:
            kw = k_ref[0, blk, pl.ds(win_tok, N_WIN), :]
            vw = v_ref[0, blk, pl.ds(win_tok, N_WIN), :]
            ckb = ck_ref[0, :, cols].astype(BF16)
            cvb = cv_ref[0, :, cols].astype(BF16)
        else:
            kw = k_ref[0, blk]
            vw = v_ref[0, blk]
        acc = jnp.zeros((TILE, MXU_N), F32)
        for hh in range(HEADS_PER_BLK):
            head = blk * HEADS_PER_BLK + hh
            lo = hh * HEAD_DIM
            hm = jnp.where((lane_row >= lo) & (lane_row < lo + HEAD_DIM), 1.0, 0.0).astype(BF16)
            qh = qb * hm
            s = _dot_t(qh, kw)
            if latent:
                parts = []
                for i in range(WIN_TILES):
                    e = e_ref[head, jnp.clip(i + tile_off, 0, WIN_TILES - 1)]
                    parts.append(s[:, i * MXU_N:(i + 1) * MXU_N] + e + band[:, i * MXU_N:(i + 1) * MXU_N])
                parts.append(_dot_t(qh, ckb))
                o = _softmax_pv(parts, [vw, cvb])
            else:
                o = _softmax_pv([s], [vw])
            acc = jnp.where((lane_full >= lo) & (lane_full < lo + HEAD_DIM), o, acc)
        blocks.append(acc)
    attn = jnp.concatenate(blocks, axis=1)
    y_a = (attn * jax.nn.silu(z_a)).astype(BF16)

    ln = _conv_ln(u_ref, ubuf, dww_ref, dwb_ref, lng_ref, lnb_ref, t, n_tiles)
    y_b = (jax.nn.silu(ln) * jax.nn.silu(z_b)).astype(BF16)

    m = (g_a * jnp.dot(y_a, wpa_ref[...], preferred_element_type=F32)
         + g_b * jnp.dot(y_b, wpb_ref[...], preferred_element_type=F32))
    o = jnp.dot(m.astype(BF16), wo_ref[...], preferred_element_type=F32)
    y = x + mod[:, 2 * D_MODEL:] * o
    if final:
        y = _rmsnorm(y, fg_ref[...])
    y_ref[0] = y


def _mix(x, mods, mod_row, layer, params, q, k, v, u, ctx_kv, e_tiles, final):
    rms_g, w_z, b_in, dw_w, dw_b, ln_g, ln_b, wpa, wpb, wo, fg = params
    bsz, seq, _ = x.shape
    nt = seq // TILE
    latent = ctx_kv is not None
    tok = lambda i, t: (i, t, 0)
    in_specs = [
        pl.BlockSpec((1, TILE, D_MODEL), tok),
        pl.BlockSpec((None, 1, 1, 3 * D_MODEL), lambda i, t: (layer, mod_row(i), 0, 0)),
        _layer_spec((1, D_MODEL), layer),
        _weight_spec((D_MODEL, N_ZG), layer),
        _layer_spec((1, D_IN), layer),
        pl.BlockSpec((1, N_BLK, TILE, MXU_N), lambda i, t: (i, 0, t, 0)),
        pl.BlockSpec((1, N_BLK, seq, MXU_N), lambda i, t: (i, 0, 0, 0)),
        pl.BlockSpec((1, N_BLK, seq, MXU_N), lambda i, t: (i, 0, 0, 0)),
        pl.BlockSpec((1, seq, W_B), lambda i, t: (i, 0, 0)),
    ]
    args = [x, mods, rms_g, w_z, b_in, q, k, v, u]
    if latent:
        ck, cv = ctx_kv
        past = ck.shape[2]
        ctx_spec = pl.BlockSpec((1, None, past, W_A), lambda i, t: (i, layer, 0, 0))
        in_specs += [ctx_spec, ctx_spec,
                     pl.BlockSpec((N_HEADS, WIN_TILES, TILE, MXU_N), lambda i, t: (0, 0, 0, 0),
                                  pipeline_mode=pl.Buffered(1))]
        args += [ck, cv, e_tiles]
    in_specs += [
        _layer_spec((CONV_K, W_B), layer),
        _layer_spec((1, W_B), layer),
        _layer_spec((1, W_B), layer),
        _layer_spec((1, W_B), layer),
        _weight_spec((W_A, D_MODEL), layer),
        _weight_spec((W_B, D_MODEL), layer),
        _weight_spec((D_MODEL, D_MODEL), layer),
        pl.BlockSpec((1, D_MODEL), lambda i, t: (0, 0)),
    ]
    args += [dw_w, dw_b, ln_g, ln_b, wpa, wpb, wo, fg]
    return pl.pallas_call(
        functools.partial(_mix_kernel, latent=latent, final=final, n_tiles=nt),
        grid=(bsz, nt),
        in_specs=in_specs,
        out_specs=pl.BlockSpec((1, TILE, D_MODEL), tok),
        out_shape=jax.ShapeDtypeStruct((bsz, seq, D_MODEL), F32),
        scratch_shapes=[pltpu.VMEM((TILE + 2 * HALO, W_B), F32)],
        compiler_params=pltpu.CompilerParams(
            dimension_semantics=("arbitrary", "arbitrary"), vmem_limit_bytes=VMEM_LIMIT),
        name="mix_latent" if latent else "mix_context",
    )(*args)


def kernel(x_prompt, x_sample, cache_k, cache_v, c, c_ctx, rms_g, w_ada, b_ada, w_in, b_in, rel_bias,
           dw_w, dw_b, ln_g, ln_b, w_proj_a, w_proj_b, w_out, final_g):
    dec_batch = x_sample.shape[0]
    ctx_row = dec_batch
    cstack = jnp.concatenate([c, c_ctx[None, :], jnp.zeros((MOD_ROWS - dec_batch - 1, D_MODEL), F32)], axis=0)
    mods = _mods(cstack, w_ada, b_ada).reshape(DEPTH, MOD_ROWS, 1, 3 * D_MODEL)

    glu_blk = COL_GLU // CAST_TN
    za_blk = COL_ZA // CAST_TN
    zb_blk = COL_ZB // CAST_TN
    w_p = _cast_cols(w_in, N_QKVU, lambda j: jnp.where(j < za_blk, j, j + (glu_blk - za_blk)), "cast_w_qkvu")
    w_z = _cast_cols(w_in, N_ZG, lambda j: jnp.where(j == 0, za_blk, j + (zb_blk - 1)), "cast_w_zg")
    ident = lambda j: j
    wpa = _cast_cols(w_proj_a, D_MODEL, ident, "cast_w_proj_a")
    wpb = _cast_cols(w_proj_b, D_MODEL, ident, "cast_w_proj_b")
    wo = _cast_cols(w_out, D_MODEL, ident, "cast_w_out")

    row = lambda a: a.reshape(DEPTH, 1, a.shape[-1])
    params = (row(rms_g), w_z, row(b_in), dw_w, row(dw_b), row(ln_g), row(ln_b), wpa, wpb, wo, final_g[None, :])
    past = cache_k.shape[2]
    ck = cache_k.reshape(dec_batch, DEPTH, past, W_A)
    cv = cache_v.reshape(dec_batch, DEPTH, past, W_A)
    table_flat = rel_bias.reshape(-1)

    x = x_prompt
    state = ()
    ctx_mod = lambda i: ctx_row
    for l in range(DEPTH):
        q, k, v, u, *state = _proj(x, mods, ctx_mod, l, params[0], w_p, params[2], tuple(state))
        x = _mix(x, mods, ctx_mod, l, params, q, k, v, u, None, None, final=(l == DEPTH - 1))
    bsz, seq, _ = x_prompt.shape
    state_k = state[0].reshape(bsz, DEPTH, seq, N_HEADS, HEAD_DIM)
    state_v = state[1].reshape(bsz, DEPTH, seq, N_HEADS, HEAD_DIM)

    z = x_sample
    lat_mod = lambda i: i
    for l in range(DEPTH):
        e_tiles = _bias_tiles(table_flat, l)
        q, k, v, u = _proj(z, mods, lat_mod, l, params[0], w_p, params[2], None)
        z = _mix(z, mods, lat_mod, l, params, q, k, v, u, (ck, cv), e_tiles, final=(l == DEPTH - 1))
    return (x, z, state_k, state_v)
```

```python
import functools

import jax
import jax.numpy as jnp
from jax import lax
from jax.experimental import pallas as pl
from jax.experimental.pallas import tpu as pltpu

F32 = jnp.float32
BF16 = jnp.bfloat16

D_MODEL = 1024
N_HEADS = 8
HEAD_DIM = 64
W_A = N_HEADS * HEAD_DIM
W_B = 512
CONV_K = 31
CONV_PAD = CONV_K // 2
GRID_W = 64
WIN_R = 8
WIN_C = 16
EPS = 1e-6
DEPTH = 2
N_DR = 2 * WIN_R - 1
N_DC = 2 * WIN_C - 1

COL_ZA = 3 * W_A
COL_GLU = COL_ZA + W_A
COL_ZB = COL_GLU + 2 * W_B
D_IN = COL_ZB + W_B + 2 * D_MODEL
N_QKVU = 3 * W_A + 2 * W_B
N_ZG = W_A + W_B + 2 * D_MODEL

LANES = 128
SUBLANES = 8
MXU_N = 256

ROWS_PER_TILE = 4
TILE = ROWS_PER_TILE * GRID_W
PROJ_TILE = 512
WIN_TILES = 3
N_WIN = WIN_TILES * TILE
HALO = 2 * SUBLANES
CONV_ROW_CHUNKS = 2
HEADS_PER_BLK = MXU_N // HEAD_DIM
N_BLK = N_HEADS // HEADS_PER_BLK
N_BIAS_V = ROWS_PER_TILE * WIN_TILES + ROWS_PER_TILE - 2
MOD_ROWS = 8
MOD_TN = 512
CAST_TN = 512
VMEM_LIMIT = 56 * 1024 * 1024
NEG_INF = float("-inf")


def _rmsnorm(x, g):
    return x * lax.rsqrt(jnp.mean(x * x, axis=-1, keepdims=True) + EPS) * g


def _modulated(x, mod, g):
    shift = mod[:, :D_MODEL]
    scale = mod[:, D_MODEL:2 * D_MODEL]
    return _rmsnorm(x, g) * (1.0 + scale) + shift


def _dot_t(a, b):
    return lax.dot_general(a, b, (((1,), (1,)), ((), ())), preferred_element_type=F32)


def _mod_kernel(c_ref, w_ref, b_ref, o_ref):
    s = jax.nn.silu(c_ref[...])
    o_ref[0] = jnp.dot(s.astype(BF16), w_ref[0].astype(BF16), preferred_element_type=F32) + b_ref[0]


def _mods(cstack, w_ada, b_ada):
    n = 3 * D_MODEL
    return pl.pallas_call(
        _mod_kernel,
        grid=(DEPTH, n // MOD_TN),
        in_specs=[
            pl.BlockSpec((MOD_ROWS, D_MODEL), lambda l, j: (0, 0)),
            pl.BlockSpec((1, D_MODEL, MOD_TN), lambda l, j: (l, 0, j)),
            pl.BlockSpec((1, 1, MOD_TN), lambda l, j: (l, 0, j)),
        ],
        out_specs=pl.BlockSpec((1, MOD_ROWS, MOD_TN), lambda l, j: (l, 0, j)),
        out_shape=jax.ShapeDtypeStruct((DEPTH, MOD_ROWS, n), F32),
        name="adaln_mods",
    )(cstack, w_ada, b_ada.reshape(DEPTH, 1, n))


def _cast_kernel(w_ref, o_ref):
    o_ref[...] = w_ref[...].astype(BF16)


def _cast_cols(w, n_out, src_block, name):
    depth, kdim, _ = w.shape
    return pl.pallas_call(
        _cast_kernel,
        grid=(depth, n_out // CAST_TN),
        in_specs=[pl.BlockSpec((1, kdim, CAST_TN), lambda l, j: (l, 0, src_block(j)))],
        out_specs=pl.BlockSpec((1, kdim, CAST_TN), lambda l, j: (l, 0, j)),
        out_shape=jax.ShapeDtypeStruct((depth, kdim, n_out), BF16),
        name=name,
    )(w)


def _bias_kernel(t_ref, e_ref, v_ref, *, layer):
    h = pl.program_id(0)
    lane = lax.broadcasted_iota(jnp.int32, (GRID_W, LANES), 1)
    cq = lax.broadcasted_iota(jnp.int32, (GRID_W, LANES), 0)
    ck = lane & (GRID_W - 1)
    hi = lane >= GRID_W
    c0 = jnp.clip(cq - WIN_C // 2, 0, GRID_W - WIN_C)
    ok = (ck >= c0) & (ck < c0 + WIN_C)

    lane1 = lax.broadcasted_iota(jnp.int32, (1, LANES), 1)
    rows = []
    for d in range(N_BIAS_V + 1):
        row = jnp.zeros((1, LANES), F32)
        for j in range(N_DC):
            row = jnp.where(lane1 == j, t_ref[((layer * N_HEADS + h) * N_DR + d) * N_DC + j], row)
        rows.append(jnp.broadcast_to(row, (GRID_W, LANES)))

    def toeplitz(d, lane0):
        return pltpu.roll(rows[d], (lane0 - (WIN_C - 1)) % LANES, axis=1, stride=1, stride_axis=0)

    for d in range(N_BIAS_V):
        tile = jnp.where(hi, toeplitz(d + 1, GRID_W), toeplitz(d, 0))
        v_ref[d] = jnp.where(ok, tile, NEG_INF)
    for cp in range(WIN_TILES):
        for rq in range(ROWS_PER_TILE):
            for p in range(MXU_N // LANES):
                d = ROWS_PER_TILE * cp + 2 * p - rq + (ROWS_PER_TILE - 1)
                e_ref[0, cp, rq * GRID_W:(rq + 1) * GRID_W, p * LANES:(p + 1) * LANES] = v_ref[d]


def _bias_tiles(table_flat, layer):
    return pl.pallas_call(
        functools.partial(_bias_kernel, layer=layer),
        grid=(N_HEADS,),
        in_specs=[pl.BlockSpec(memory_space=pltpu.SMEM)],
        out_specs=pl.BlockSpec((1, WIN_TILES, TILE, MXU_N), lambda h: (h, 0, 0, 0)),
        out_shape=jax.ShapeDtypeStruct((N_HEADS, WIN_TILES, TILE, MXU_N), F32),
        scratch_shapes=[pltpu.VMEM((N_BIAS_V, GRID_W, LANES), F32)],
        name="rel_bias_tiles",
    )(table_flat)


def _proj_kernel(*refs, n_state_in, all_layers):
    x_ref, mod_ref, g_ref, w_ref, b_ref = refs[:5]
    q_ref, k_ref, v_ref, u_ref, h_ref, *state_refs = refs[5 + n_state_in:]
    h = _modulated(x_ref[0], mod_ref[0], g_ref[...]).astype(BF16)
    h_ref[0] = h
    bias = jnp.concatenate([b_ref[:, :COL_ZA], b_ref[:, COL_GLU:COL_ZB]], axis=1)
    p = jnp.dot(h, w_ref[...], preferred_element_type=F32) + bias
    q = (p[:, :W_A] * (HEAD_DIM ** -0.5)).astype(BF16)
    k = p[:, W_A:2 * W_A]
    v = p[:, 2 * W_A:3 * W_A]
    k16 = k.astype(BF16)
    v16 = v.astype(BF16)
    for blk in range(N_BLK):
        cols = slice(blk * MXU_N, (blk + 1) * MXU_N)
        q_ref[0, blk] = q[:, cols]
        k_ref[0, blk] = k16[:, cols]
        v_ref[0, blk] = v16[:, cols]
    a = p[:, 3 * W_A:3 * W_A + W_B]
    gate = p[:, 3 * W_A + W_B:]
    u_ref[0] = a * jax.nn.sigmoid(gate)
    if state_refs:
        for d in (range(DEPTH) if all_layers else range(1)):
            state_refs[0][0, d] = k
            state_refs[1][0, d] = v


def _layer_spec(shape, layer):
    return pl.BlockSpec((None,) + tuple(shape), lambda i, t: (layer,) + (0,) * len(shape))


def _weight_spec(shape, layer):
    return pl.BlockSpec((None,) + tuple(shape), lambda i, t: (layer,) + (0,) * len(shape),
                        pipeline_mode=pl.Buffered(1))


def _proj(x, mods, mod_row, layer, rms_g, w_p, b_in, state):
    bsz, seq, _ = x.shape
    tile = min(PROJ_TILE, seq)
    tok = lambda i, t: (i, t, 0)
    blk_tok = lambda i, t: (i, 0, t, 0)
    out_shape = ([jax.ShapeDtypeStruct((bsz, N_BLK, seq, MXU_N), BF16)] * 3
                 + [jax.ShapeDtypeStruct((bsz, seq, W_B), F32), jax.ShapeDtypeStruct((bsz, seq, D_MODEL), BF16)])
    out_specs = ([pl.BlockSpec((1, N_BLK, tile, MXU_N), blk_tok)] * 3
                 + [pl.BlockSpec((1, tile, W_B), tok), pl.BlockSpec((1, tile, D_MODEL), tok)])
    n_out = len(out_specs)
    in_specs = [
        pl.BlockSpec((1, tile, D_MODEL), tok),
        pl.BlockSpec((None, 1, 1, 3 * D_MODEL), lambda i, t: (layer, mod_row(i), 0, 0)),
        _layer_spec((1, D_MODEL), layer),
        _weight_spec((D_MODEL, N_QKVU), layer),
        _layer_spec((1, D_IN), layer),
    ]
    args = [x, mods, rms_g, w_p, b_in]
    aliases = {}
    if state is not None:
        out_shape += [jax.ShapeDtypeStruct((bsz, DEPTH, seq, W_A), F32)] * 2
        if state:
            in_specs += [pl.BlockSpec(memory_space=pl.ANY)] * 2
            aliases = {len(args): n_out, len(args) + 1: n_out + 1}
            args += list(state)
            out_specs += [pl.BlockSpec((1, 1, tile, W_A), lambda i, t: (i, layer, t, 0))] * 2
        else:
            out_specs += [pl.BlockSpec((1, DEPTH, tile, W_A), lambda i, t: (i, 0, t, 0))] * 2
    return pl.pallas_call(
        functools.partial(_proj_kernel, n_state_in=len(state or ()), all_layers=(state == ())),
        grid=(bsz, seq // tile),
        in_specs=in_specs,
        out_specs=out_specs,
        out_shape=out_shape,
        input_output_aliases=aliases,
        compiler_params=pltpu.CompilerParams(
            dimension_semantics=("arbitrary", "arbitrary"), vmem_limit_bytes=VMEM_LIMIT),
        name="proj_qkvu",
    )(*args)


def _softmax_pv(s_parts, v_parts):
    s = jnp.concatenate(s_parts, axis=1) if len(s_parts) > 1 else s_parts[0]
    m = jnp.max(s, axis=1, keepdims=True)
    p = jnp.exp(s - m)
    l = jnp.sum(p, axis=1, keepdims=True)
    p = p.astype(BF16)
    o = None
    start = 0
    for vp in v_parts:
        n = vp.shape[0]
        part = jnp.dot(p[:, start:start + n], vp, preferred_element_type=F32)
        o = part if o is None else o + part
        start += n
    return o / l


def _conv_ln(u_ref, ubuf, dww_ref, dwb_ref, lng_ref, lnb_ref, t, n_tiles):
    if n_tiles == 1:
        ubuf[0:HALO, :] = jnp.zeros((HALO, W_B), F32)
        ubuf[HALO:HALO + TILE, :] = u_ref[0]
        ubuf[HALO + TILE:, :] = jnp.zeros((HALO, W_B), F32)
    else:
        t0 = pl.multiple_of(t * TILE, TILE)
        left = u_ref[0, pl.ds(pl.multiple_of(jnp.maximum(t0 - HALO, 0), HALO), HALO), :]
        right = u_ref[0, pl.ds(pl.multiple_of(jnp.minimum(t0 + TILE, n_tiles * TILE - HALO), HALO), HALO), :]
        ubuf[0:HALO, :] = jnp.where(t > 0, left, 0.0)
        ubuf[HALO:HALO + TILE, :] = u_ref[0, pl.ds(t0, TILE), :]
        ubuf[HALO + TILE:, :] = jnp.where(t < n_tiles - 1, right, 0.0)
    half = TILE // CONV_ROW_CHUNKS
    span = half + 4 * SUBLANES
    conv_cols = []
    for c in range(W_B // LANES):
        lanes = slice(c * LANES, (c + 1) * LANES)
        halves = []
        for hf in range(CONV_ROW_CHUNKS):
            acc = jnp.zeros((half, LANES), F32)
            base = ubuf[hf * half:hf * half + span, lanes]
            for rr in range(SUBLANES):
                sh = base if rr == 0 else pltpu.roll(base, span - rr, axis=0)
                for j in range(4):
                    kk = SUBLANES * j + rr - (HALO - CONV_PAD)
                    if 0 <= kk < CONV_K:
                        acc = acc + dww_ref[kk:kk + 1, lanes] * sh[SUBLANES * j:SUBLANES * j + half]
            halves.append(acc)
        conv_cols.append(jnp.concatenate(halves, axis=0))
    cv = jnp.concatenate(conv_cols, axis=1) + dwb_ref[...]
    mu = jnp.mean(cv, axis=-1, keepdims=True)
    var = jnp.mean(jnp.square(cv - mu), axis=-1, keepdims=True)
    return (cv - mu) * lax.rsqrt(var + EPS) * lng_ref[...] + lnb_ref[...]


def _mix_kernel(*refs, latent, final, n_tiles):
    if latent:
        (x_ref, mod_ref, h_ref, wz_ref, b_ref, q_ref, k_ref, v_ref, u_ref, ck_ref, cv_ref, e_ref,
         dww_ref, dwb_ref, lng_ref, lnb_ref, wpa_ref, wpb_ref, wo_ref, fg_ref, y_ref, ubuf) = refs
    else:
        (x_ref, mod_ref, h_ref, wz_ref, b_ref, q_ref, k_ref, v_ref, u_ref,
         dww_ref, dwb_ref, lng_ref, lnb_ref, wpa_ref, wpb_ref, wo_ref, fg_ref, y_ref, ubuf) = refs
    t = pl.program_id(1)
    x = x_ref[0]
    mod = mod_ref[0]
    bias = jnp.concatenate([b_ref[:, COL_ZA:COL_GLU], b_ref[:, COL_ZB:]], axis=1)
    zg = jnp.dot(h_ref[0], wz_ref[...], preferred_element_type=F32) + bias
    z_a = zg[:, :W_A]
    z_b = zg[:, W_A:W_A + W_B]
    g_a = jax.nn.sigmoid(zg[:, W_A + W_B:W_A + W_B + D_MODEL])
    g_b = jax.nn.sigmoid(zg[:, W_A + W_B + D_MODEL:])

    lane_row = lax.broadcasted_iota(jnp.int32, (1, MXU_N), 1)
    lane_full = lax.broadcasted_iota(jnp.int32, (TILE, MXU_N), 1)
    if latent:
        first_row = ROWS_PER_TILE * t
        win_row = jnp.clip(first_row - ROWS_PER_TILE, 0, (n_tiles - WIN_TILES) * ROWS_PER_TILE)
        win_tok = pl.multiple_of(win_row * GRID_W, TILE)
        tile_off = (win_row - (first_row - ROWS_PER_TILE)) // ROWS_PER_TILE
        r = first_row + (lax.broadcasted_iota(jnp.int32, (TILE, N_WIN), 0) >> 6)
        kr = win_row + (lax.broadcasted_iota(jnp.int32, (TILE, N_WIN), 1) >> 6)
        r0 = jnp.clip(r - WIN_R // 2, 0, n_tiles * ROWS_PER_TILE - WIN_R)
        band = jnp.where((kr >= r0) & (kr < r0 + WIN_R), 0.0, NEG_INF)
    blocks = []
    for blk in range(N_BLK):
        cols = slice(blk * MXU_N, (blk + 1) * MXU_N)
        qb = q_ref[0, blk]
        if latent:
            kw = k_ref[0, blk, pl.ds(win_tok, N_WIN), :]
            vw = v_ref[0, blk, pl.ds(win_tok, N_WIN), :]
            ckb = ck_ref[0, :, cols].astype(BF16)
            cvb = cv_ref[0, :, cols].astype(BF16)
        else:
            kw = k_ref[0, blk]
            vw = v_ref[0, blk]
        acc = jnp.zeros((TILE, MXU_N), F32)
        for hh in range(HEADS_PER_BLK):
            head = blk * HEADS_PER_BLK + hh
            lo = hh * HEAD_DIM
            hm = jnp.where((lane_row >= lo) & (lane_row < lo + HEAD_DIM), 1.0, 0.0).astype(BF16)
            qh = qb * hm
            s = _dot_t(qh, kw)
            if latent:
                parts = []
                for i in range(WIN_TILES):
                    e = e_ref[head, jnp.clip(i + tile_off, 0, WIN_TILES - 1)]
                    parts.append(s[:, i * MXU_N:(i + 1) * MXU_N] + e + band[:, i * MXU_N:(i + 1) * MXU_N])
                parts.append(_dot_t(qh, ckb))
                o = _softmax_pv(parts, [vw, cvb])
            else:
                o = _softmax_pv([s], [vw])
            acc = jnp.where((lane_full >= lo) & (lane_full < lo + HEAD_DIM), o, acc)
        blocks.append(acc)
    attn = jnp.concatenate(blocks, axis=1)
    y_a = (attn * jax.nn.silu(z_a)).astype(BF16)

    ln = _conv_ln(u_ref, ubuf, dww_ref, dwb_ref, lng_ref, lnb_ref, t, n_tiles)
    y_b = (jax.nn.silu(ln) * jax.nn.silu(z_b)).astype(BF16)

    m = (g_a * jnp.dot(y_a, wpa_ref[...], preferred_element_type=F32)
         + g_b * jnp.dot(y_b, wpb_ref[...], preferred_element_type=F32))
    o = jnp.dot(m.astype(BF16), wo_ref[...], preferred_element_type=F32)
    y = x + mod[:, 2 * D_MODEL:] * o
    if final:
        y = _rmsnorm(y, fg_ref[...])
    y_ref[0] = y


def _mix(x, mods, mod_row, layer, params, q, k, v, u, h, ctx_kv, e_tiles, final):
    _, w_z, b_in, dw_w, dw_b, ln_g, ln_b, wpa, wpb, wo, fg = params
    bsz, seq, _ = x.shape
    nt = seq // TILE
    latent = ctx_kv is not None
    tok = lambda i, t: (i, t, 0)
    in_specs = [
        pl.BlockSpec((1, TILE, D_MODEL), tok),
        pl.BlockSpec((None, 1, 1, 3 * D_MODEL), lambda i, t: (layer, mod_row(i), 0, 0)),
        pl.BlockSpec((1, TILE, D_MODEL), tok),
        _weight_spec((D_MODEL, N_ZG), layer),
        _layer_spec((1, D_IN), layer),
        pl.BlockSpec((1, N_BLK, TILE, MXU_N), lambda i, t: (i, 0, t, 0)),
        pl.BlockSpec((1, N_BLK, seq, MXU_N), lambda i, t: (i, 0, 0, 0)),
        pl.BlockSpec((1, N_BLK, seq, MXU_N), lambda i, t: (i, 0, 0, 0)),
        pl.BlockSpec((1, seq, W_B), lambda i, t: (i, 0, 0)),
    ]
    args = [x, mods, h, w_z, b_in, q, k, v, u]
    if latent:
        ck, cv = ctx_kv
        past = ck.shape[2]
        ctx_spec = pl.BlockSpec((1, None, past, W_A), lambda i, t: (i, layer, 0, 0))
        in_specs += [ctx_spec, ctx_spec,
                     pl.BlockSpec((N_HEADS, WIN_TILES, TILE, MXU_N), lambda i, t: (0, 0, 0, 0),
                                  pipeline_mode=pl.Buffered(1))]
        args += [ck, cv, e_tiles]
    in_specs += [
        _layer_spec((CONV_K, W_B), layer),
        _layer_spec((1, W_B), layer),
        _layer_spec((1, W_B), layer),
        _layer_spec((1, W_B), layer),
        _weight_spec((W_A, D_MODEL), layer),
        _weight_spec((W_B, D_MODEL), layer),
        _weight_spec((D_MODEL, D_MODEL), layer),
        pl.BlockSpec((1, D_MODEL), lambda i, t: (0, 0)),
    ]
    args += [dw_w, dw_b, ln_g, ln_b, wpa, wpb, wo, fg]
    return pl.pallas_call(
        functools.partial(_mix_kernel, latent=latent, final=final, n_tiles=nt),
        grid=(bsz, nt),
        in_specs=in_specs,
        out_specs=pl.BlockSpec((1, TILE, D_MODEL), tok),
        out_shape=jax.ShapeDtypeStruct((bsz, seq, D_MODEL), F32),
        scratch_shapes=[pltpu.VMEM((TILE + 2 * HALO, W_B), F32)],
        compiler_params=pltpu.CompilerParams(
            dimension_semantics=("arbitrary", "arbitrary"), vmem_limit_bytes=VMEM_LIMIT),
        name="mix_latent" if latent else "mix_context",
    )(*args)


def kernel(x_prompt, x_sample, cache_k, cache_v, c, c_ctx, rms_g, w_ada, b_ada, w_in, b_in, rel_bias,
           dw_w, dw_b, ln_g, ln_b, w_proj_a, w_proj_b, w_out, final_g):
    dec_batch = x_sample.shape[0]
    ctx_row = dec_batch
    cstack = jnp.concatenate([c, c_ctx[None, :], jnp.zeros((MOD_ROWS - dec_batch - 1, D_MODEL), F32)], axis=0)
    mods = _mods(cstack, w_ada, b_ada).reshape(DEPTH, MOD_ROWS, 1, 3 * D_MODEL)

    glu_blk = COL_GLU // CAST_TN
    za_blk = COL_ZA // CAST_TN
    zb_blk = COL_ZB // CAST_TN
    w_p = _cast_cols(w_in, N_QKVU, lambda j: jnp.where(j < za_blk, j, j + (glu_blk - za_blk)), "cast_w_qkvu")
    w_z = _cast_cols(w_in, N_ZG, lambda j: jnp.where(j == 0, za_blk, j + (zb_blk - 1)), "cast_w_zg")
    ident = lambda j: j
    wpa = _cast_cols(w_proj_a, D_MODEL, ident, "cast_w_proj_a")
    wpb = _cast_cols(w_proj_b, D_MODEL, ident, "cast_w_proj_b")
    wo = _cast_cols(w_out, D_MODEL, ident, "cast_w_out")

    row = lambda a: a.reshape(DEPTH, 1, a.shape[-1])
    params = (row(rms_g), w_z, row(b_in), dw_w, row(dw_b), row(ln_g), row(ln_b), wpa, wpb, wo, final_g[None, :])
    past = cache_k.shape[2]
    ck = cache_k.reshape(dec_batch, DEPTH, past, W_A)
    cv = cache_v.reshape(dec_batch, DEPTH, past, W_A)
    table_flat = rel_bias.reshape(-1)

    x = x_prompt
    state = ()
    ctx_mod = lambda i: ctx_row
    for l in range(DEPTH):
        q, k, v, u, h, *state = _proj(x, mods, ctx_mod, l, params[0], w_p, params[2], tuple(state))
        x = _mix(x, mods, ctx_mod, l, params, q, k, v, u, h, None, None, final=(l == DEPTH - 1))
    bsz, seq, _ = x_prompt.shape
    state_k = state[0].reshape(bsz, DEPTH, seq, N_HEADS, HEAD_DIM)
    state_v = state[1].reshape(bsz, DEPTH, seq, N_HEADS, HEAD_DIM)

    z = x_sample
    lat_mod = lambda i: i
    for l in range(DEPTH):
        e_tiles = _bias_tiles(table_flat, l)
        q, k, v, u, h = _proj(z, mods, lat_mod, l, params[0], w_p, params[2], None)
        z = _mix(z, mods, lat_mod, l, params, q, k, v, u, h, (ck, cv), e_tiles, final=(l == DEPTH - 1))
    return (x, z, state_k, state_v)
```

```python
import functools

import jax
import jax.numpy as jnp
from jax import lax
from jax.experimental import pallas as pl
from jax.experimental.pallas import tpu as pltpu

F32 = jnp.float32
BF16 = jnp.bfloat16

D_MODEL = 1024
N_HEADS = 8
HEAD_DIM = 64
W_A = N_HEADS * HEAD_DIM
W_B = 512
CONV_K = 31
CONV_PAD = CONV_K // 2
GRID_W = 64
WIN_R = 8
WIN_C = 16
EPS = 1e-6
DEPTH = 2
N_DR = 2 * WIN_R - 1
N_DC = 2 * WIN_C - 1

COL_ZA = 3 * W_A
COL_GLU = COL_ZA + W_A
COL_ZB = COL_GLU + 2 * W_B
D_IN = COL_ZB + W_B + 2 * D_MODEL
N_QKVU = 3 * W_A + 2 * W_B
N_ZG = W_A + W_B + 2 * D_MODEL

LANES = 128
SUBLANES = 8
MXU_N = 256

ROWS_PER_TILE = 4
TILE = ROWS_PER_TILE * GRID_W
PROJ_TILE = 512
WIN_TILES = 3
N_WIN = WIN_TILES * TILE
HALO = 2 * SUBLANES
CONV_ROW_CHUNKS = 4
HEADS_PER_BLK = MXU_N // HEAD_DIM
N_BLK = N_HEADS // HEADS_PER_BLK
N_BIAS_V = ROWS_PER_TILE * WIN_TILES + ROWS_PER_TILE - 2
MOD_ROWS = 8
MOD_TN = 512
CAST_TN = 512
VMEM_LIMIT = 56 * 1024 * 1024
NEG_INF = float("-inf")


def _rmsnorm(x, g):
    return x * lax.rsqrt(jnp.mean(x * x, axis=-1, keepdims=True) + EPS) * g


def _modulated(x, mod, g):
    shift = mod[:, :D_MODEL]
    scale = mod[:, D_MODEL:2 * D_MODEL]
    return _rmsnorm(x, g) * (1.0 + scale) + shift


def _dot_t(a, b):
    return lax.dot_general(a, b, (((1,), (1,)), ((), ())), preferred_element_type=F32)


def _mod_kernel(c_ref, w_ref, b_ref, o_ref):
    s = jax.nn.silu(c_ref[...])
    o_ref[0] = jnp.dot(s.astype(BF16), w_ref[0].astype(BF16), preferred_element_type=F32) + b_ref[0]


def _mods(cstack, w_ada, b_ada):
    n = 3 * D_MODEL
    return pl.pallas_call(
        _mod_kernel,
        grid=(DEPTH, n // MOD_TN),
        in_specs=[
            pl.BlockSpec((MOD_ROWS, D_MODEL), lambda l, j: (0, 0)),
            pl.BlockSpec((1, D_MODEL, MOD_TN), lambda l, j: (l, 0, j)),
            pl.BlockSpec((1, 1, MOD_TN), lambda l, j: (l, 0, j)),
        ],
        out_specs=pl.BlockSpec((1, MOD_ROWS, MOD_TN), lambda l, j: (l, 0, j)),
        out_shape=jax.ShapeDtypeStruct((DEPTH, MOD_ROWS, n), F32),
        name="adaln_mods",
    )(cstack, w_ada, b_ada.reshape(DEPTH, 1, n))


def _cast_kernel(w_ref, o_ref):
    o_ref[...] = w_ref[...].astype(BF16)


def _cast_cols(w, n_out, src_block, name):
    depth, kdim, _ = w.shape
    return pl.pallas_call(
        _cast_kernel,
        grid=(depth, n_out // CAST_TN),
        in_specs=[pl.BlockSpec((1, kdim, CAST_TN), lambda l, j: (l, 0, src_block(j)))],
        out_specs=pl.BlockSpec((1, kdim, CAST_TN), lambda l, j: (l, 0, j)),
        out_shape=jax.ShapeDtypeStruct((depth, kdim, n_out), BF16),
        name=name,
    )(w)


def _bias_kernel(t_ref, e_ref, v_ref, *, layer):
    h = pl.program_id(0)
    lane = lax.broadcasted_iota(jnp.int32, (GRID_W, LANES), 1)
    cq = lax.broadcasted_iota(jnp.int32, (GRID_W, LANES), 0)
    ck = lane & (GRID_W - 1)
    hi = lane >= GRID_W
    c0 = jnp.clip(cq - WIN_C // 2, 0, GRID_W - WIN_C)
    ok = (ck >= c0) & (ck < c0 + WIN_C)

    lane1 = lax.broadcasted_iota(jnp.int32, (1, LANES), 1)
    rows = []
    for d in range(N_BIAS_V + 1):
        row = jnp.zeros((1, LANES), F32)
        for j in range(N_DC):
            row = jnp.where(lane1 == j, t_ref[((layer * N_HEADS + h) * N_DR + d) * N_DC + j], row)
        rows.append(jnp.broadcast_to(row, (GRID_W, LANES)))

    def toeplitz(d, lane0):
        return pltpu.roll(rows[d], (lane0 - (WIN_C - 1)) % LANES, axis=1, stride=1, stride_axis=0)

    for d in range(N_BIAS_V):
        tile = jnp.where(hi, toeplitz(d + 1, GRID_W), toeplitz(d, 0))
        v_ref[d] = jnp.where(ok, tile, NEG_INF)
    for cp in range(WIN_TILES):
        for rq in range(ROWS_PER_TILE):
            for p in range(MXU_N // LANES):
                d = ROWS_PER_TILE * cp + 2 * p - rq + (ROWS_PER_TILE - 1)
                e_ref[0, cp, rq * GRID_W:(rq + 1) * GRID_W, p * LANES:(p + 1) * LANES] = v_ref[d]


def _bias_tiles(table_flat, layer):
    return pl.pallas_call(
        functools.partial(_bias_kernel, layer=layer),
        grid=(N_HEADS,),
        in_specs=[pl.BlockSpec(memory_space=pltpu.SMEM)],
        out_specs=pl.BlockSpec((1, WIN_TILES, TILE, MXU_N), lambda h: (h, 0, 0, 0)),
        out_shape=jax.ShapeDtypeStruct((N_HEADS, WIN_TILES, TILE, MXU_N), F32),
        scratch_shapes=[pltpu.VMEM((N_BIAS_V, GRID_W, LANES), F32)],
        name="rel_bias_tiles",
    )(table_flat)


def _proj_kernel(*refs, n_state_in, all_layers):
    x_ref, mod_ref, g_ref, w_ref, b_ref = refs[:5]
    q_ref, k_ref, v_ref, u_ref, h_ref, *state_refs = refs[5 + n_state_in:]
    h = _modulated(x_ref[0], mod_ref[0], g_ref[...]).astype(BF16)
    h_ref[0] = h
    bias = jnp.concatenate([b_ref[:, :COL_ZA], b_ref[:, COL_GLU:COL_ZB]], axis=1)
    p = jnp.dot(h, w_ref[...], preferred_element_type=F32) + bias
    q = (p[:, :W_A] * (HEAD_DIM ** -0.5)).astype(BF16)
    k = p[:, W_A:2 * W_A]
    v = p[:, 2 * W_A:3 * W_A]
    k16 = k.astype(BF16)
    v16 = v.astype(BF16)
    for blk in range(N_BLK):
        cols = slice(blk * MXU_N, (blk + 1) * MXU_N)
        q_ref[0, blk] = q[:, cols]
        k_ref[0, blk] = k16[:, cols]
        v_ref[0, blk] = v16[:, cols]
    a = p[:, 3 * W_A:3 * W_A + W_B]
    gate = p[:, 3 * W_A + W_B:]
    u_ref[0] = a * jax.nn.sigmoid(gate)
    if state_refs:
        for d in (range(DEPTH) if all_layers else range(1)):
            state_refs[0][0, d] = k
            state_refs[1][0, d] = v


def _layer_spec(shape, layer):
    return pl.BlockSpec((None,) + tuple(shape), lambda i, t: (layer,) + (0,) * len(shape))


def _weight_spec(shape, layer):
    return pl.BlockSpec((None,) + tuple(shape), lambda i, t: (layer,) + (0,) * len(shape),
                        pipeline_mode=pl.Buffered(1))


def _proj(x, mods, mod_row, layer, rms_g, w_p, b_in, state):
    bsz, seq, _ = x.shape
    tile = min(PROJ_TILE, seq)
    tok = lambda i, t: (i, t, 0)
    blk_tok = lambda i, t: (i, 0, t, 0)
    out_shape = ([jax.ShapeDtypeStruct((bsz, N_BLK, seq, MXU_N), BF16)] * 3
                 + [jax.ShapeDtypeStruct((bsz, seq, W_B), F32), jax.ShapeDtypeStruct((bsz, seq, D_MODEL), BF16)])
    out_specs = ([pl.BlockSpec((1, N_BLK, tile, MXU_N), blk_tok)] * 3
                 + [pl.BlockSpec((1, tile, W_B), tok), pl.BlockSpec((1, tile, D_MODEL), tok)])
    n_out = len(out_specs)
    in_specs = [
        pl.BlockSpec((1, tile, D_MODEL), tok),
        pl.BlockSpec((None, 1, 1, 3 * D_MODEL), lambda i, t: (layer, mod_row(i), 0, 0)),
        _layer_spec((1, D_MODEL), layer),
        _weight_spec((D_MODEL, N_QKVU), layer),
        _layer_spec((1, D_IN), layer),
    ]
    args = [x, mods, rms_g, w_p, b_in]
    aliases = {}
    if state is not None:
        out_shape += [jax.ShapeDtypeStruct((bsz, DEPTH, seq, W_A), F32)] * 2
        if state:
            in_specs += [pl.BlockSpec(memory_space=pl.ANY)] * 2
            aliases = {len(args): n_out, len(args) + 1: n_out + 1}
            args += list(state)
            out_specs += [pl.BlockSpec((1, 1, tile, W_A), lambda i, t: (i, layer, t, 0))] * 2
        else:
            out_specs += [pl.BlockSpec((1, DEPTH, tile, W_A), lambda i, t: (i, 0, t, 0))] * 2
    return pl.pallas_call(
        functools.partial(_proj_kernel, n_state_in=len(state or ()), all_layers=(state == ())),
        grid=(bsz, seq // tile),
        in_specs=in_specs,
        out_specs=out_specs,
        out_shape=out_shape,
        input_output_aliases=aliases,
        compiler_params=pltpu.CompilerParams(
            dimension_semantics=("arbitrary", "arbitrary"), vmem_limit_bytes=VMEM_LIMIT),
        name="proj_qkvu",
    )(*args)


def _softmax_pv(s_parts, v_parts):
    s = jnp.concatenate(s_parts, axis=1) if len(s_parts) > 1 else s_parts[0]
    m = jnp.max(s, axis=1, keepdims=True)
    p = jnp.exp(s - m)
    l = jnp.sum(p, axis=1, keepdims=True)
    p = p.astype(BF16)
    o = None
    start = 0
    for vp in v_parts:
        n = vp.shape[0]
        part = jnp.dot(p[:, start:start + n], vp, preferred_element_type=F32)
        o = part if o is None else o + part
        start += n
    return o / l


def _conv_ln_and_zg(u_ref, ubuf, dww_ref, dwb_ref, lng_ref, lnb_ref, h_ref, wz_ref, bias, t, n_tiles):
    if n_tiles == 1:
        ubuf[0:HALO, :] = jnp.zeros((HALO, W_B), F32)
        ubuf[HALO:HALO + TILE, :] = u_ref[0]
        ubuf[HALO + TILE:, :] = jnp.zeros((HALO, W_B), F32)
    else:
        t0 = pl.multiple_of(t * TILE, TILE)
        left = u_ref[0, pl.ds(pl.multiple_of(jnp.maximum(t0 - HALO, 0), HALO), HALO), :]
        right = u_ref[0, pl.ds(pl.multiple_of(jnp.minimum(t0 + TILE, n_tiles * TILE - HALO), HALO), HALO), :]
        ubuf[0:HALO, :] = jnp.where(t > 0, left, 0.0)
        ubuf[HALO:HALO + TILE, :] = u_ref[0, pl.ds(t0, TILE), :]
        ubuf[HALO + TILE:, :] = jnp.where(t < n_tiles - 1, right, 0.0)
    half = TILE // CONV_ROW_CHUNKS
    span = half + 4 * SUBLANES
    n_units = (W_B // LANES) * CONV_ROW_CHUNKS
    n_ztiles = N_ZG // MXU_N
    h16 = h_ref[0]
    units, ztiles = [], []

    def conv_unit(k):
        c, hf = divmod(k, CONV_ROW_CHUNKS)
        lanes = slice(c * LANES, (c + 1) * LANES)
        acc = jnp.zeros((half, LANES), F32)
        dep = (k * n_ztiles) // n_units - 2
        if dep >= 0:
            acc = acc + _held(ztiles[dep][0:half, 0:LANES])
        base = ubuf[hf * half:hf * half + span, lanes]
        for rr in range(SUBLANES):
            sh = base if rr == 0 else pltpu.roll(base, span - rr, axis=0)
            for j in range(4):
                kk = SUBLANES * j + rr - (HALO - CONV_PAD)
                if 0 <= kk < CONV_K:
                    acc = acc + dww_ref[kk:kk + 1, lanes] * sh[SUBLANES * j:SUBLANES * j + half]
        units.append(acc)

    def proj_tile(j):
        cols = slice(j * MXU_N, (j + 1) * MXU_N)
        lhs = h16
        dep = (j * n_units) // n_ztiles - 1
        if dep >= 0:
            z0 = _held(units[dep][0:2 * SUBLANES, :]).astype(BF16)
            top = h16[0:2 * SUBLANES] + jnp.concatenate([z0] * (D_MODEL // LANES), axis=1)
            lhs = jnp.concatenate([top, h16[2 * SUBLANES:]], axis=0)
        ztiles.append(jnp.dot(lhs, wz_ref[:, cols], preferred_element_type=F32) + bias[:, cols])

    events = sorted([(k * n_ztiles, 0, k) for k in range(n_units)] + [(j * n_units, 1, j) for j in range(n_ztiles)])
    for _, is_tile, idx in events:
        (proj_tile if is_tile else conv_unit)(idx)
    cv = jnp.concatenate(
        [jnp.concatenate(units[c * CONV_ROW_CHUNKS:(c + 1) * CONV_ROW_CHUNKS], axis=0) for c in range(W_B // LANES)],
        axis=1) + dwb_ref[...]
    mu = jnp.mean(cv, axis=-1, keepdims=True)
    var = jnp.mean(jnp.square(cv - mu), axis=-1, keepdims=True)
    ln = (cv - mu) * lax.rsqrt(var + EPS) * lng_ref[...] + lnb_ref[...]
    return ln, jnp.concatenate(ztiles, axis=1)


def _held(x):
    bits = pltpu.bitcast(x, jnp.uint32)
    bits = lax.shift_right_logical(lax.shift_right_logical(bits, jnp.uint32(16)), jnp.uint32(16))
    return pltpu.bitcast(bits, F32)


def _mix_kernel(*refs, latent, final, n_tiles):
    if latent:
        (x_ref, mod_ref, h_ref, wz_ref, b_ref, q_ref, k_ref, v_ref, u_ref, ck_ref, cv_ref, e_ref,
         dww_ref, dwb_ref, lng_ref, lnb_ref, wpa_ref, wpb_ref, wo_ref, fg_ref, y_ref, ubuf) = refs
    else:
        (x_ref, mod_ref, h_ref, wz_ref, b_ref, q_ref, k_ref, v_ref, u_ref,
         dww_ref, dwb_ref, lng_ref, lnb_ref, wpa_ref, wpb_ref, wo_ref, fg_ref, y_ref, ubuf) = refs
    t = pl.program_id(1)
    x = x_ref[0]
    mod = mod_ref[0]
    bias = jnp.concatenate([b_ref[:, COL_ZA:COL_GLU], b_ref[:, COL_ZB:]], axis=1)
    ln, zg = _conv_ln_and_zg(u_ref, ubuf, dww_ref, dwb_ref, lng_ref, lnb_ref, h_ref, wz_ref, bias, t, n_tiles)
    z_a = zg[:, :W_A]
    z_b = zg[:, W_A:W_A + W_B]
    g_a = jax.nn.sigmoid(zg[:, W_A + W_B:W_A + W_B + D_MODEL])
    g_b = jax.nn.sigmoid(zg[:, W_A + W_B + D_MODEL:])

    lane_row = lax.broadcasted_iota(jnp.int32, (1, MXU_N), 1)
    lane_full = lax.broadcasted_iota(jnp.int32, (TILE, MXU_N), 1)
    if latent:
        first_row = ROWS_PER_TILE * t
        win_row = jnp.clip(first_row - ROWS_PER_TILE, 0, (n_tiles - WIN_TILES) * ROWS_PER_TILE)
        win_tok = pl.multiple_of(win_row * GRID_W, TILE)
        tile_off = (win_row - (first_row - ROWS_PER_TILE)) // ROWS_PER_TILE
        r = first_row + (lax.broadcasted_iota(jnp.int32, (TILE, N_WIN), 0) >> 6)
        kr = win_row + (lax.broadcasted_iota(jnp.int32, (TILE, N_WIN), 1) >> 6)
        r0 = jnp.clip(r - WIN_R // 2, 0, n_tiles * ROWS_PER_TILE - WIN_R)
        band = jnp.where((kr >= r0) & (kr < r0 + WIN_R), 0.0, NEG_INF)
    blocks = []
    for blk in range(N_BLK):
        cols = slice(blk * MXU_N, (blk + 1) * MXU_N)
        qb = q_ref[0, blk]
        if latent:
            kw = k_ref[0, blk, pl.ds(win_tok, N_WIN), :]
            vw = v_ref[0, blk, pl.ds(win_tok, N_WIN), :]
            ckb = ck_ref[0, :, cols].astype(BF16)
            cvb = cv_ref[0, :, cols].astype(BF16)
        else:
            kw = k_ref[0, blk]
            vw = v_ref[0, blk]
        acc = jnp.zeros((TILE, MXU_N), F32)
        for hh in range(HEADS_PER_BLK):
            head = blk * HEADS_PER_BLK + hh
            lo = hh * HEAD_DIM
            hm = jnp.where((lane_row >= lo) & (lane_row < lo + HEAD_DIM), 1.0, 0.0).astype(BF16)
            qh = qb * hm
            s = _dot_t(qh, kw)
            if latent:
                parts = []
                for i in range(WIN_TILES):
                    e = e_ref[head, jnp.clip(i + tile_off, 0, WIN_TILES - 1)]
                    parts.append(s[:, i * MXU_N:(i + 1) * MXU_N] + e + band[:, i * MXU_N:(i + 1) * MXU_N])
                parts.append(_dot_t(qh, ckb))
                o = _softmax_pv(parts, [vw, cvb])
            else:
                o = _softmax_pv([s], [vw])
            acc = jnp.where((lane_full >= lo) & (lane_full < lo + HEAD_DIM), o, acc)
        blocks.append(acc)
    attn = jnp.concatenate(blocks, axis=1)
    y_a = (attn * jax.nn.silu(z_a)).astype(BF16)

    y_b =(jax.nn.silu(ln) * jax.nn.silu(z_b)).astype(BF16)

    m = (g_a * jnp.dot(y_a, wpa_ref[...], preferred_element_type=F32)
         + g_b * jnp.dot(y_b, wpb_ref[...], preferred_element_type=F32))
    o = jnp.dot(m.astype(BF16), wo_ref[...], preferred_element_type=F32)
    y = x + mod[:, 2 * D_MODEL:] * o
    if final:
        y = _rmsnorm(y, fg_ref[...])
    y_ref[0] = y


def _mix(x, mods, mod_row, layer, params, q, k, v, u, h, ctx_kv, e_tiles, final):
    _, w_z, b_in, dw_w, dw_b, ln_g, ln_b, wpa, wpb, wo, fg = params
    bsz, seq, _ = x.shape
    nt = seq // TILE
    latent = ctx_kv is not None
    tok = lambda i, t: (i, t, 0)
    in_specs = [
        pl.BlockSpec((1, TILE, D_MODEL), tok),
        pl.BlockSpec((None, 1, 1, 3 * D_MODEL), lambda i, t: (layer, mod_row(i), 0, 0)),
        pl.BlockSpec((1, TILE, D_MODEL), tok),
        _weight_spec((D_MODEL, N_ZG), layer),
        _layer_spec((1, D_IN), layer),
        pl.BlockSpec((1, N_BLK, TILE, MXU_N), lambda i, t: (i, 0, t, 0)),
        pl.BlockSpec((1, N_BLK, seq, MXU_N), lambda i, t: (i, 0, 0, 0)),
        pl.BlockSpec((1, N_BLK, seq, MXU_N), lambda i, t: (i, 0, 0, 0)),
        pl.BlockSpec((1, seq, W_B), lambda i, t: (i, 0, 0)),
    ]
    args = [x, mods, h, w_z, b_in, q, k, v, u]
    if latent:
        ck, cv = ctx_kv
        past = ck.shape[2]
        ctx_spec = pl.BlockSpec((1, None, past, W_A), lambda i, t: (i, layer, 0, 0))
        in_specs += [ctx_spec, ctx_spec,
                     pl.BlockSpec((N_HEADS, WIN_TILES, TILE, MXU_N), lambda i, t: (0, 0, 0, 0),
                                  pipeline_mode=pl.Buffered(1))]
        args += [ck, cv, e_tiles]
    in_specs += [
        _layer_spec((CONV_K, W_B), layer),
        _layer_spec((1, W_B), layer),
        _layer_spec((1, W_B), layer),
        _layer_spec((1, W_B), layer),
        _weight_spec((W_A, D_MODEL), layer),
        _weight_spec((W_B, D_MODEL), layer),
        _weight_spec((D_MODEL, D_MODEL), layer),
        pl.BlockSpec((1, D_MODEL), lambda i, t: (0, 0)),
    ]
    args += [dw_w, dw_b, ln_g, ln_b, wpa, wpb, wo, fg]
    return pl.pallas_call(
        functools.partial(_mix_kernel, latent=latent, final=final, n_tiles=nt),
        grid=(bsz, nt),
        in_specs=in_specs,
        out_specs=pl.BlockSpec((1, TILE, D_MODEL), tok),
        out_shape=jax.ShapeDtypeStruct((bsz, seq, D_MODEL), F32),
        scratch_shapes=[pltpu.VMEM((TILE + 2 * HALO, W_B), F32)],
        compiler_params=pltpu.CompilerParams(
            dimension_semantics=("arbitrary", "arbitrary"), vmem_limit_bytes=VMEM_LIMIT),
        name="mix_latent" if latent else "mix_context",
    )(*args)


def kernel(x_prompt, x_sample, cache_k, cache_v, c, c_ctx, rms_g, w_ada, b_ada, w_in, b_in, rel_bias,
           dw_w, dw_b, ln_g, ln_b, w_proj_a, w_proj_b, w_out, final_g):
    dec_batch = x_sample.shape[0]
    ctx_row = dec_batch
    cstack = jnp.concatenate([c, c_ctx[None, :], jnp.zeros((MOD_ROWS - dec_batch - 1, D_MODEL), F32)], axis=0)
    mods = _mods(cstack, w_ada, b_ada).reshape(DEPTH, MOD_ROWS, 1, 3 * D_MODEL)

    glu_blk = COL_GLU // CAST_TN
    za_blk = COL_ZA // CAST_TN
    zb_blk = COL_ZB // CAST_TN
    w_p = _cast_cols(w_in, N_QKVU, lambda j: jnp.where(j < za_blk, j, j + (glu_blk - za_blk)), "cast_w_qkvu")
    w_z = _cast_cols(w_in, N_ZG, lambda j: jnp.where(j == 0, za_blk, j + (zb_blk - 1)), "cast_w_zg")
    ident = lambda j: j
    wpa = _cast_cols(w_proj_a, D_MODEL, ident, "cast_w_proj_a")
    wpb = _cast_cols(w_proj_b, D_MODEL, ident, "cast_w_proj_b")
    wo = _cast_cols(w_out, D_MODEL, ident, "cast_w_out")

    row = lambda a: a.reshape(DEPTH, 1, a.shape[-1])
    params = (row(rms_g), w_z, row(b_in), dw_w, row(dw_b), row(ln_g), row(ln_b), wpa, wpb, wo, final_g[None, :])
    past = cache_k.shape[2]
    ck = cache_k.reshape(dec_batch, DEPTH, past, W_A)
    cv = cache_v.reshape(dec_batch, DEPTH, past, W_A)
    table_flat = rel_bias.reshape(-1)

    x = x_prompt
    state = ()
    ctx_mod = lambda i: ctx_row
    for l in range(DEPTH):
        q, k, v, u, h, *state = _proj(x, mods, ctx_mod, l, params[0], w_p, params[2], tuple(state))
        x = _mix(x, mods, ctx_mod, l, params, q, k, v, u, h, None, None, final=(l == DEPTH - 1))
    bsz, seq, _ = x_prompt.shape
    state_k = state[0].reshape(bsz, DEPTH, seq, N_HEADS, HEAD_DIM)
    state_v = state[1].reshape(bsz, DEPTH, seq, N_HEADS, HEAD_DIM)

    z = x_sample
    lat_mod = lambda i: i
    for l in range(DEPTH):
        e_tiles = _bias_tiles(table_flat, l)
        q, k, v, u, h = _proj(z, mods, lat_mod, l, params[0], w_p, params[2], None)
        z = _mix(z, mods, lat_mod, l, params, q, k, v, u, h, (ck, cv), e_tiles, final=(l == DEPTH - 1))
    return (x, z, state_k, state_v)
```

```python
import functools

import jax
import jax.numpy as jnp
from jax import lax
from jax.experimental import pallas as pl
from jax.experimental.pallas import tpu as pltpu

F32 = jnp.float32
BF16 = jnp.bfloat16

D_MODEL = 1024
N_HEADS = 8
HEAD_DIM = 64
W_A = N_HEADS * HEAD_DIM
W_B = 512
CONV_K = 31
CONV_PAD = CONV_K // 2
GRID_W = 64
WIN_R = 8
WIN_C = 16
EPS = 1e-6
DEPTH = 2
N_DR = 2 * WIN_R - 1
N_DC = 2 * WIN_C - 1

COL_ZA = 3 * W_A
COL_GLU = COL_ZA + W_A
COL_ZB = COL_GLU + 2 * W_B
D_IN = COL_ZB + W_B + 2 * D_MODEL
N_QKVU = 3 * W_A + 2 * W_B
N_ZG = W_A + W_B + 2 * D_MODEL

LANES = 128
SUBLANES = 8
MXU_N = 256

ROWS_PER_TILE = 4
TILE = ROWS_PER_TILE * GRID_W
PROJ_TILE = 512
WIN_TILES = 3
N_WIN = WIN_TILES * TILE
HALO = 2 * SUBLANES
CONV_ROW_CHUNKS = 4
HEADS_PER_BLK = MXU_N // HEAD_DIM
N_BLK = N_HEADS // HEADS_PER_BLK
N_BIAS_V = ROWS_PER_TILE * WIN_TILES + ROWS_PER_TILE - 2
MOD_ROWS = 8
MOD_TN = 512
CAST_TN = 512
VMEM_LIMIT = 56 * 1024 * 1024
NEG_INF = float("-inf")


def _rmsnorm(x, g):
    return x * lax.rsqrt(jnp.mean(x * x, axis=-1, keepdims=True) + EPS) * g


def _modulated(x, mod, g):
    shift = mod[:, :D_MODEL]
    scale = mod[:, D_MODEL:2 * D_MODEL]
    return _rmsnorm(x, g) * (1.0 + scale) + shift


def _dot_t(a, b):
    return lax.dot_general(a, b, (((1,), (1,)), ((), ())), preferred_element_type=F32)


def _mod_kernel(c_ref, w_ref, b_ref, o_ref):
    s = jax.nn.silu(c_ref[...])
    o_ref[0] = jnp.dot(s.astype(BF16), w_ref[0].astype(BF16), preferred_element_type=F32) + b_ref[0]


def _mods(cstack, w_ada, b_ada):
    n = 3 * D_MODEL
    return pl.pallas_call(
        _mod_kernel,
        grid=(DEPTH, n // MOD_TN),
        in_specs=[
            pl.BlockSpec((MOD_ROWS, D_MODEL), lambda l, j: (0, 0)),
            pl.BlockSpec((1, D_MODEL, MOD_TN), lambda l, j: (l, 0, j)),
            pl.BlockSpec((1, 1, MOD_TN), lambda l, j: (l, 0, j)),
        ],
        out_specs=pl.BlockSpec((1, MOD_ROWS, MOD_TN), lambda l, j: (l, 0, j)),
        out_shape=jax.ShapeDtypeStruct((DEPTH, MOD_ROWS, n), F32),
        name="adaln_mods",
    )(cstack, w_ada, b_ada.reshape(DEPTH, 1, n))


def _cast_kernel(w_ref, o_ref):
    o_ref[...] = w_ref[...].astype(BF16)


def _cast_cols(w, n_out, src_block, name):
    depth, kdim, _ = w.shape
    return pl.pallas_call(
        _cast_kernel,
        grid=(depth, n_out // CAST_TN),
        in_specs=[pl.BlockSpec((1, kdim, CAST_TN), lambda l, j: (l, 0, src_block(j)))],
        out_specs=pl.BlockSpec((1, kdim, CAST_TN), lambda l, j: (l, 0, j)),
        out_shape=jax.ShapeDtypeStruct((depth, kdim, n_out), BF16),
        name=name,
    )(w)


def _cast_many_kernel(*refs):
    n = len(refs) // 2
    for w_ref, o_ref in zip(refs[:n], refs[n:]):
        o_ref[...] = w_ref[...].astype(BF16)


def _cast_many(ws, name):
    depth, _, n = ws[0].shape
    spec = lambda w: pl.BlockSpec((1, w.shape[1], CAST_TN), lambda l, j: (l, 0, j))
    return pl.pallas_call(
        _cast_many_kernel,
        grid=(depth, n // CAST_TN),
        in_specs=[spec(w) for w in ws],
        out_specs=[spec(w) for w in ws],
        out_shape=[jax.ShapeDtypeStruct(w.shape, BF16) for w in ws],
        name=name,
    )(*ws)


def _bias_kernel(t_ref, e_ref, v_ref, *, layer):
    h = pl.program_id(0)
    lane = lax.broadcasted_iota(jnp.int32, (GRID_W, LANES), 1)
    cq = lax.broadcasted_iota(jnp.int32, (GRID_W, LANES), 0)
    ck = lane & (GRID_W - 1)
    hi = lane >= GRID_W
    c0 = jnp.clip(cq - WIN_C // 2, 0, GRID_W - WIN_C)
    ok = (ck >= c0) & (ck < c0 + WIN_C)

    lane1 = lax.broadcasted_iota(jnp.int32, (1, LANES), 1)
    rows = []
    for d in range(N_BIAS_V + 1):
        row = jnp.zeros((1, LANES), F32)
        for j in range(N_DC):
            row = jnp.where(lane1 == j, t_ref[((layer * N_HEADS + h) * N_DR + d) * N_DC + j], row)
        rows.append(jnp.broadcast_to(row, (GRID_W, LANES)))

    def toeplitz(d, lane0):
        return pltpu.roll(rows[d], (lane0 - (WIN_C - 1)) % LANES, axis=1, stride=1, stride_axis=0)

    for d in range(N_BIAS_V):
        tile = jnp.where(hi, toeplitz(d + 1, GRID_W), toeplitz(d, 0))
        v_ref[d] = jnp.where(ok, tile, NEG_INF)
    for cp in range(WIN_TILES):
        for rq in range(ROWS_PER_TILE):
            for p in range(MXU_N // LANES):
                d = ROWS_PER_TILE * cp + 2 * p - rq + (ROWS_PER_TILE - 1)
                e_ref[0, cp, rq * GRID_W:(rq + 1) * GRID_W, p * LANES:(p + 1) * LANES] = v_ref[d]


def _bias_tiles(table_flat, layer):
    return pl.pallas_call(
        functools.partial(_bias_kernel, layer=layer),
        grid=(N_HEADS,),
        in_specs=[pl.BlockSpec(memory_space=pltpu.SMEM)],
        out_specs=pl.BlockSpec((1, WIN_TILES, TILE, MXU_N), lambda h: (h, 0, 0, 0)),
        out_shape=jax.ShapeDtypeStruct((N_HEADS, WIN_TILES, TILE, MXU_N), F32),
        scratch_shapes=[pltpu.VMEM((N_BIAS_V, GRID_W, LANES), F32)],
        name="rel_bias_tiles",
    )(table_flat)


def _proj_kernel(*refs, n_state_in, all_layers):
    x_ref, mod_ref, g_ref, w_ref, b_ref = refs[:5]
    q_ref, k_ref, v_ref, u_ref, h_ref, *state_refs = refs[5 + n_state_in:]
    h = _modulated(x_ref[0], mod_ref[0], g_ref[...]).astype(BF16)
    h_ref[0] = h
    bias = jnp.concatenate([b_ref[:, :COL_ZA], b_ref[:, COL_GLU:COL_ZB]], axis=1)
    p = jnp.dot(h, w_ref[...], preferred_element_type=F32) + bias
    q = (p[:, :W_A] * (HEAD_DIM ** -0.5)).astype(BF16)
    k = p[:, W_A:2 * W_A]
    v = p[:, 2 * W_A:3 * W_A]
    k16 = k.astype(BF16)
    v16 = v.astype(BF16)
    for blk in range(N_BLK):
        cols = slice(blk * MXU_N, (blk + 1) * MXU_N)
        q_ref[0, blk] = q[:, cols]
        k_ref[0, blk] = k16[:, cols]
        v_ref[0, blk] = v16[:, cols]
    a = p[:, 3 * W_A:3 * W_A + W_B]
    gate = p[:, 3 * W_A + W_B:]
    u_ref[0] = a * jax.nn.sigmoid(gate)
    if state_refs:
        for d in (range(DEPTH) if all_layers else range(1)):
            state_refs[0][0, d] = k
            state_refs[1][0, d] = v


def _layer_spec(shape, layer):
    return pl.BlockSpec((None,) + tuple(shape), lambda i, t: (layer,) + (0,) * len(shape))


def _weight_spec(shape, layer):
    return pl.BlockSpec((None,) + tuple(shape), lambda i, t: (layer,) + (0,) * len(shape),
                        pipeline_mode=pl.Buffered(1))


def _proj(x, mods, mod_row, layer, rms_g, w_p, b_in, state):
    bsz, seq, _ = x.shape
    tile = min(PROJ_TILE, seq)
    tok = lambda i, t: (i, t, 0)
    blk_tok = lambda i, t: (i, 0, t, 0)
    out_shape = ([jax.ShapeDtypeStruct((bsz, N_BLK, seq, MXU_N), BF16)] * 3
                 + [jax.ShapeDtypeStruct((bsz, seq, W_B), F32), jax.ShapeDtypeStruct((bsz, seq, D_MODEL), BF16)])
    out_specs = ([pl.BlockSpec((1, N_BLK, tile, MXU_N), blk_tok)] * 3
                 + [pl.BlockSpec((1, tile, W_B), tok), pl.BlockSpec((1, tile, D_MODEL), tok)])
    n_out = len(out_specs)
    in_specs = [
        pl.BlockSpec((1, tile, D_MODEL), tok),
        pl.BlockSpec((None, 1, 1, 3 * D_MODEL), lambda i, t: (layer, mod_row(i), 0, 0)),
        _layer_spec((1, D_MODEL), layer),
        _weight_spec((D_MODEL, N_QKVU), layer),
        _layer_spec((1, D_IN), layer),
    ]
    args = [x, mods, rms_g, w_p, b_in]
    aliases = {}
    if state is not None:
        out_shape += [jax.ShapeDtypeStruct((bsz, DEPTH, seq, W_A), F32)] * 2
        if state:
            in_specs += [pl.BlockSpec(memory_space=pl.ANY)] * 2
            aliases = {len(args): n_out, len(args) + 1: n_out + 1}
            args += list(state)
            out_specs += [pl.BlockSpec((1, 1, tile, W_A), lambda i, t: (i, layer, t, 0))] * 2
        else:
            out_specs += [pl.BlockSpec((1, DEPTH, tile, W_A), lambda i, t: (i, 0, t, 0))] * 2
    return pl.pallas_call(
        functools.partial(_proj_kernel, n_state_in=len(state or ()), all_layers=(state == ())),
        grid=(bsz, seq // tile),
        in_specs=in_specs,
        out_specs=out_specs,
        out_shape=out_shape,
        input_output_aliases=aliases,
        compiler_params=pltpu.CompilerParams(
            dimension_semantics=("arbitrary", "arbitrary"), vmem_limit_bytes=VMEM_LIMIT),
        name="proj_qkvu",
    )(*args)


def _softmax_pv(s_parts, v_parts):
    s = jnp.concatenate(s_parts, axis=1) if len(s_parts) > 1 else s_parts[0]
    m = jnp.max(s, axis=1, keepdims=True)
    p = jnp.exp(s - m)
    l = jnp.sum(p, axis=1, keepdims=True)
    p = p.astype(BF16)
    o = None
    start = 0
    for vp in v_parts:
        n = vp.shape[0]
        part = jnp.dot(p[:, start:start + n], vp, preferred_element_type=F32)
        o = part if o is None else o + part
        start += n
    return o / l


def _conv_ln_and_zg(u_ref, ubuf, dww_ref, dwb_ref, lng_ref, lnb_ref, h_ref, wz_ref, bias, t, n_tiles):
    if n_tiles == 1:
        ubuf[0:HALO, :] = jnp.zeros((HALO, W_B), F32)
        ubuf[HALO:HALO + TILE, :] = u_ref[0]
        ubuf[HALO + TILE:, :] = jnp.zeros((HALO, W_B), F32)
    else:
        t0 = pl.multiple_of(t * TILE, TILE)
        left = u_ref[0, pl.ds(pl.multiple_of(jnp.maximum(t0 - HALO, 0), HALO), HALO), :]
        right = u_ref[0, pl.ds(pl.multiple_of(jnp.minimum(t0 + TILE, n_tiles * TILE - HALO), HALO), HALO), :]
        ubuf[0:HALO, :] = jnp.where(t > 0, left, 0.0)
        ubuf[HALO:HALO + TILE, :] = u_ref[0, pl.ds(t0, TILE), :]
        ubuf[HALO + TILE:, :] = jnp.where(t < n_tiles - 1, right, 0.0)
    half = TILE // CONV_ROW_CHUNKS
    span = half + 4 * SUBLANES
    n_units = (W_B // LANES) * CONV_ROW_CHUNKS
    n_ztiles = N_ZG // MXU_N
    h16 = h_ref[0]
    units, ztiles = [], []

    def conv_unit(k):
        c, hf = divmod(k, CONV_ROW_CHUNKS)
        lanes = slice(c * LANES, (c + 1) * LANES)
        acc = jnp.zeros((half, LANES), F32)
        dep = (k * n_ztiles) // n_units - 3
        if dep >= 0:
            acc = acc + _held(ztiles[dep][0:half, 0:LANES])
        base = ubuf[hf * half:hf * half + span, lanes]
        for rr in range(SUBLANES):
            sh = base if rr == 0 else pltpu.roll(base, span - rr, axis=0)
            for j in range(4):
                kk = SUBLANES * j + rr - (HALO - CONV_PAD)
                if 0 <= kk < CONV_K:
                    acc = acc + dww_ref[kk:kk + 1, lanes] * sh[SUBLANES * j:SUBLANES * j + half]
        units.append(acc)

    def proj_tile(j):
        cols = slice(j * MXU_N, (j + 1) * MXU_N)
        lhs = h16
        dep = (j * n_units) // n_ztiles - 1
        if dep >= 0:
            z0 = _held(units[dep][0:2 * SUBLANES, :]).astype(BF16)
            top = h16[0:2 * SUBLANES] + jnp.concatenate([z0] * (D_MODEL // LANES), axis=1)
            lhs = jnp.concatenate([top, h16[2 * SUBLANES:]], axis=0)
        ztiles.append(jnp.dot(lhs, wz_ref[:, cols], preferred_element_type=F32) + bias[:, cols])

    events = sorted([(k * n_ztiles, 0, k) for k in range(n_units)] + [(j * n_units, 1, j) for j in range(n_ztiles)])
    for _, is_tile, idx in events:
        (proj_tile if is_tile else conv_unit)(idx)
    cv = jnp.concatenate(
        [jnp.concatenate(units[c * CONV_ROW_CHUNKS:(c + 1) * CONV_ROW_CHUNKS], axis=0) for c in range(W_B // LANES)],
        axis=1) + dwb_ref[...]
    mu = jnp.mean(cv, axis=-1, keepdims=True)
    var = jnp.mean(jnp.square(cv - mu), axis=-1, keepdims=True)
    ln = (cv - mu) * lax.rsqrt(var + EPS) * lng_ref[...] + lnb_ref[...]
    return ln, jnp.concatenate(ztiles, axis=1)


def _held(x):
    bits = pltpu.bitcast(x, jnp.uint32)
    bits = lax.shift_right_logical(lax.shift_right_logical(bits, jnp.uint32(16)), jnp.uint32(16))
    return pltpu.bitcast(bits, F32)


def _mix_kernel(*refs, latent, final, n_tiles):
    if latent:
        (x_ref, mod_ref, h_ref, wz_ref, b_ref, q_ref, k_ref, v_ref, u_ref, ck_ref, cv_ref, e_ref,
         dww_ref, dwb_ref, lng_ref, lnb_ref, wpa_ref, wpb_ref, wo_ref, fg_ref, y_ref, ubuf) = refs
    else:
        (x_ref, mod_ref, h_ref, wz_ref, b_ref, q_ref, k_ref, v_ref, u_ref,
         dww_ref, dwb_ref, lng_ref, lnb_ref, wpa_ref, wpb_ref, wo_ref, fg_ref, y_ref, ubuf) = refs
    t = pl.program_id(1)
    x = x_ref[0]
    mod = mod_ref[0]
    bias = jnp.concatenate([b_ref[:, COL_ZA:COL_GLU], b_ref[:, COL_ZB:]], axis=1)
    ln, zg = _conv_ln_and_zg(u_ref, ubuf, dww_ref, dwb_ref, lng_ref, lnb_ref, h_ref, wz_ref, bias, t, n_tiles)
    z_a = zg[:, :W_A]
    z_b = zg[:, W_A:W_A + W_B]
    g_a = jax.nn.sigmoid(zg[:, W_A + W_B:W_A + W_B + D_MODEL])
    g_b = jax.nn.sigmoid(zg[:, W_A + W_B + D_MODEL:])

    lane_row = lax.broadcasted_iota(jnp.int32, (1, MXU_N), 1)
    lane_full = lax.broadcasted_iota(jnp.int32, (TILE, MXU_N), 1)
    if latent:
        first_row = ROWS_PER_TILE * t
        win_row = jnp.clip(first_row - ROWS_PER_TILE, 0, (n_tiles - WIN_TILES) * ROWS_PER_TILE)
        win_tok = pl.multiple_of(win_row * GRID_W, TILE)
        tile_off = (win_row - (first_row - ROWS_PER_TILE)) // ROWS_PER_TILE
        r = first_row + (lax.broadcasted_iota(jnp.int32, (TILE, N_WIN), 0) >> 6)
        kr = win_row + (lax.broadcasted_iota(jnp.int32, (TILE, N_WIN), 1) >> 6)
        r0 = jnp.clip(r - WIN_R // 2, 0, n_tiles * ROWS_PER_TILE - WIN_R)
        band = jnp.where((kr >= r0) & (kr < r0 + WIN_R), 0.0, NEG_INF)
    blocks = []
    for blk in range(N_BLK):
        cols = slice(blk * MXU_N, (blk + 1) * MXU_N)
        qb = q_ref[0, blk]
        if latent:
            kw = k_ref[0, blk, pl.ds(win_tok, N_WIN), :]
            vw = v_ref[0, blk, pl.ds(win_tok, N_WIN), :]
            ckb = ck_ref[0, :, cols].astype(BF16)
            cvb = cv_ref[0, :, cols].astype(BF16)
        else:
            kw = k_ref[0, blk]
            vw = v_ref[0, blk]
        acc = jnp.zeros((TILE, MXU_N), F32)
        for hh in range(HEADS_PER_BLK):
            head = blk * HEADS_PER_BLK + hh
            lo = hh * HEAD_DIM
            hm = jnp.where((lane_row >= lo) & (lane_row < lo + HEAD_DIM), 1.0, 0.0).astype(BF16)
            qh = qb * hm
            s = _dot_t(qh, kw)
            if latent:
                parts = []
                for i in range(WIN_TILES):
                    e = e_ref[head, jnp.clip(i + tile_off, 0, WIN_TILES - 1)]
                    parts.append(s[:, i * MXU_N:(i + 1) * MXU_N] + e + band[:, i * MXU_N:(i + 1) * MXU_N])
                parts.append(_dot_t(qh, ckb))
                o = _softmax_pv(parts, [vw, cvb])
            else:
                o = _softmax_pv([s], [vw])
            acc = jnp.where((lane_full >= lo) & (lane_full < lo + HEAD_DIM), o, acc)
        blocks.append(acc)
    attn = jnp.concatenate(blocks, axis=1)
    y_a = (attn * jax.nn.silu(z_a)).astype(BF16)

    y_b =(jax.nn.silu(ln) * jax.nn.silu(z_b)).astype(BF16)

    m = (g_a * jnp.dot(y_a, wpa_ref[...], preferred_element_type=F32)
         + g_b * jnp.dot(y_b, wpb_ref[...], preferred_element_type=F32))
    o = jnp.dot(m.astype(BF16), wo_ref[...], preferred_element_type=F32)
    y = x + mod[:, 2 * D_MODEL:] * o
    if final:
        y = _rmsnorm(y, fg_ref[...])
    y_ref[0] = y


def _mix(x, mods, mod_row, layer, params, q, k, v, u, h, ctx_kv, e_tiles, final):
    _, w_z, b_in, dw_w, dw_b, ln_g, ln_b, wpa, wpb, wo, fg = params
    bsz, seq, _ = x.shape
    nt = seq // TILE
    latent = ctx_kv is not None
    tok = lambda i, t: (i, t, 0)
    in_specs = [
        pl.BlockSpec((1, TILE, D_MODEL), tok),
        pl.BlockSpec((None, 1, 1, 3 * D_MODEL), lambda i, t: (layer, mod_row(i), 0, 0)),
        pl.BlockSpec((1, TILE, D_MODEL), tok),
        _weight_spec((D_MODEL, N_ZG), layer),
        _layer_spec((1, D_IN), layer),
        pl.BlockSpec((1, N_BLK, TILE, MXU_N), lambda i, t: (i, 0, t, 0)),
        pl.BlockSpec((1, N_BLK, seq, MXU_N), lambda i, t: (i, 0, 0, 0)),
        pl.BlockSpec((1, N_BLK, seq, MXU_N), lambda i, t: (i, 0, 0, 0)),
        pl.BlockSpec((1, seq, W_B), lambda i, t: (i, 0, 0)),
    ]
    args = [x, mods, h, w_z, b_in, q, k, v, u]
    if latent:
        ck, cv = ctx_kv
        past = ck.shape[2]
        ctx_spec = pl.BlockSpec((1, None, past, W_A), lambda i, t: (i, layer, 0, 0))
        in_specs += [ctx_spec, ctx_spec,
                     pl.BlockSpec((N_HEADS, WIN_TILES, TILE, MXU_N), lambda i, t: (0, 0, 0, 0),
                                  pipeline_mode=pl.Buffered(1))]
        args += [ck, cv, e_tiles]
    in_specs += [
        _layer_spec((CONV_K, W_B), layer),
        _layer_spec((1, W_B), layer),
        _layer_spec((1, W_B), layer),
        _layer_spec((1, W_B), layer),
        _weight_spec((W_A, D_MODEL), layer),
        _weight_spec((W_B, D_MODEL), layer),
        _weight_spec((D_MODEL, D_MODEL), layer),
        pl.BlockSpec((1, D_MODEL), lambda i, t: (0, 0)),
    ]
    args += [dw_w, dw_b, ln_g, ln_b, wpa, wpb, wo, fg]
    return pl.pallas_call(
        functools.partial(_mix_kernel, latent=latent, final=final, n_tiles=nt),
        grid=(bsz, nt),
        in_specs=in_specs,
        out_specs=pl.BlockSpec((1, TILE, D_MODEL), tok),
        out_shape=jax.ShapeDtypeStruct((bsz, seq, D_MODEL), F32),
        scratch_shapes=[pltpu.VMEM((TILE + 2 * HALO, W_B), F32)],
        compiler_params=pltpu.CompilerParams(
            dimension_semantics=("arbitrary", "arbitrary"), vmem_limit_bytes=VMEM_LIMIT),
        name="mix_latent" if latent else "mix_context",
    )(*args)


def kernel(x_prompt, x_sample, cache_k, cache_v, c, c_ctx, rms_g, w_ada, b_ada, w_in, b_in, rel_bias,
           dw_w, dw_b, ln_g, ln_b, w_proj_a, w_proj_b, w_out, final_g):
    dec_batch = x_sample.shape[0]
    ctx_row = dec_batch
    cstack = jnp.concatenate([c, c_ctx[None, :], jnp.zeros((MOD_ROWS - dec_batch - 1, D_MODEL), F32)], axis=0)
    mods = _mods(cstack, w_ada, b_ada).reshape(DEPTH, MOD_ROWS, 1, 3 * D_MODEL)

    glu_blk = COL_GLU // CAST_TN
    za_blk = COL_ZA // CAST_TN
    zb_blk = COL_ZB // CAST_TN
    w_p = _cast_cols(w_in, N_QKVU, lambda j: jnp.where(j < za_blk, j, j + (glu_blk - za_blk)), "cast_w_qkvu")
    w_z = _cast_cols(w_in, N_ZG, lambda j: jnp.where(j == 0, za_blk, j + (zb_blk - 1)), "cast_w_zg")
    wpa, wpb, wo = _cast_many([w_proj_a, w_proj_b, w_out], "cast_w_merge")

    row = lambda a: a.reshape(DEPTH, 1, a.shape[-1])
    params = (row(rms_g), w_z, row(b_in), dw_w, row(dw_b), row(ln_g), row(ln_b), wpa, wpb, wo, final_g[None, :])
    past = cache_k.shape[2]
    ck = cache_k.reshape(dec_batch, DEPTH, past, W_A)
    cv = cache_v.reshape(dec_batch, DEPTH, past, W_A)
    table_flat = rel_bias.reshape(-1)

    x = x_prompt
    state = ()
    ctx_mod = lambda i: ctx_row
    for l in range(DEPTH):
        q, k, v, u, h, *state = _proj(x, mods, ctx_mod, l, params[0], w_p, params[2], tuple(state))
        x = _mix(x, mods, ctx_mod, l, params, q, k, v, u, h, None, None, final=(l == DEPTH - 1))
    bsz, seq, _ = x_prompt.shape
    state_k = state[0].reshape(bsz, DEPTH, seq, N_HEADS, HEAD_DIM)
    state_v = state[1].reshape(bsz, DEPTH, seq, N_HEADS, HEAD_DIM)

    z = x_sample
    lat_mod = lambda i: i
    for l in range(DEPTH):
        e_tiles = _bias_tiles(table_flat, l)
        q, k, v, u, h = _proj(z, mods, lat_mod, l, params[0], w_p, params[2], None)
        z = _mix(z, mods, lat_mod, l, params, q, k, v, u, h, (ck, cv), e_tiles, final=(l == DEPTH - 1))
    return (x, z, state_k, state_v)
```

```python
import functools

import jax
import jax.numpy as jnp
from jax import lax
from jax.experimental import pallas as pl
from jax.experimental.pallas import tpu as pltpu

F32 = jnp.float32
BF16 = jnp.bfloat16

D_MODEL = 1024
N_HEADS = 8
HEAD_DIM = 64
W_A = N_HEADS * HEAD_DIM
W_B = 512
CONV_K = 31
CONV_PAD = CONV_K // 2
GRID_W = 64
WIN_R = 8
WIN_C = 16
EPS = 1e-6
DEPTH = 2
N_DR = 2 * WIN_R - 1
N_DC = 2 * WIN_C - 1

COL_ZA = 3 * W_A
COL_GLU = COL_ZA + W_A
COL_ZB = COL_GLU + 2 * W_B
D_IN = COL_ZB + W_B + 2 * D_MODEL
N_QKVU = 3 * W_A + 2 * W_B
N_ZG = W_A + W_B + 2 * D_MODEL

LANES = 128
SUBLANES = 8
MXU_N = 256

ROWS_PER_TILE = 4
TILE = ROWS_PER_TILE * GRID_W
PROJ_TILE = 512
WIN_TILES = 3
N_WIN = WIN_TILES * TILE
HALO = 2 * SUBLANES
CONV_ROW_CHUNKS = 4
HEADS_PER_BLK = MXU_N // HEAD_DIM
N_BLK = N_HEADS // HEADS_PER_BLK
N_BIAS_V = ROWS_PER_TILE * WIN_TILES + ROWS_PER_TILE - 2
N_BAND_VARIANTS = 3
MOD_ROWS = 8
MOD_TN = 1024
CAST_TN = 512
VMEM_LIMIT = 58 * 1024 * 1024
NEG_INF = float("-inf")


def _rmsnorm(x, g):
    return x * lax.rsqrt(jnp.mean(x * x, axis=-1, keepdims=True) + EPS) * g


def _modulated(x, mod, g):
    shift = mod[:, :D_MODEL]
    scale = mod[:, D_MODEL:2 * D_MODEL]
    return _rmsnorm(x, g) * (1.0 + scale) + shift


def _dot_t(a, b):
    return lax.dot_general(a, b, (((1,), (1,)), ((), ())), preferred_element_type=F32)


def _mod_kernel(c_ref, w_ref, b_ref, o_ref):
    s = jax.nn.silu(c_ref[...])
    o_ref[0] = jnp.dot(s.astype(BF16), w_ref[0].astype(BF16), preferred_element_type=F32) + b_ref[0]


def _mods(cstack, w_ada, b_ada):
    n = 3 * D_MODEL
    return pl.pallas_call(
        _mod_kernel,
        grid=(DEPTH, n // MOD_TN),
        in_specs=[
            pl.BlockSpec((MOD_ROWS, D_MODEL), lambda l, j: (0, 0)),
            pl.BlockSpec((1, D_MODEL, MOD_TN), lambda l, j: (l, 0, j)),
            pl.BlockSpec((1, 1, MOD_TN), lambda l, j: (l, 0, j)),
        ],
        out_specs=pl.BlockSpec((1, MOD_ROWS, MOD_TN), lambda l, j: (l, 0, j)),
        out_shape=jax.ShapeDtypeStruct((DEPTH, MOD_ROWS, n), F32),
        name="adaln_mods",
    )(cstack, w_ada, b_ada.reshape(DEPTH, 1, n))


def _cast_kernel(w_ref, o_ref):
    o_ref[...] = w_ref[...].astype(BF16)


def _cast_cols(w, n_out, src_block, name):
    depth, kdim, _ = w.shape
    return pl.pallas_call(
        _cast_kernel,
        grid=(depth, n_out // CAST_TN),
        in_specs=[pl.BlockSpec((1, kdim, CAST_TN), lambda l, j: (l, 0, src_block(j)))],
        out_specs=pl.BlockSpec((1, kdim, CAST_TN), lambda l, j: (l, 0, j)),
        out_shape=jax.ShapeDtypeStruct((depth, kdim, n_out), BF16),
        name=name,
    )(w)


def _cast_many_kernel(*refs):
    n = len(refs) // 2
    for w_ref, o_ref in zip(refs[:n], refs[n:]):
        o_ref[...] = w_ref[...].astype(BF16)


def _cast_many(ws, name):
    depth, _, n = ws[0].shape
    spec = lambda w: pl.BlockSpec((1, w.shape[1], CAST_TN), lambda l, j: (l, 0, j))
    return pl.pallas_call(
        _cast_many_kernel,
        grid=(depth, n // CAST_TN),
        in_specs=[spec(w) for w in ws],
        out_specs=[spec(w) for w in ws],
        out_shape=[jax.ShapeDtypeStruct(w.shape, BF16) for w in ws],
        name=name,
    )(*ws)


def _bias_kernel(t_ref, e_ref, v_ref, *, layer):
    h = pl.program_id(0)
    lane = lax.broadcasted_iota(jnp.int32, (GRID_W, LANES), 1)
    cq = lax.broadcasted_iota(jnp.int32, (GRID_W, LANES), 0)
    ck = lane & (GRID_W - 1)
    hi = lane >= GRID_W
    c0 = jnp.clip(cq - WIN_C // 2, 0, GRID_W - WIN_C)
    ok = (ck >= c0) & (ck < c0 + WIN_C)

    lane1 = lax.broadcasted_iota(jnp.int32, (1, LANES), 1)
    rows = []
    for d in range(N_BIAS_V + 1):
        row = jnp.zeros((1, LANES), F32)
        for j in range(N_DC):
            row = jnp.where(lane1 == j, t_ref[((layer * N_HEADS + h) * N_DR + d) * N_DC + j], row)
        rows.append(jnp.broadcast_to(row, (GRID_W, LANES)))

    def toeplitz(d, lane0):
        return pltpu.roll(rows[d], (lane0 - (WIN_C - 1)) % LANES, axis=1, stride=1, stride_axis=0)

    for d in range(N_BIAS_V):
        tile = jnp.where(hi, toeplitz(d + 1, GRID_W), toeplitz(d, 0))
        v_ref[d] = jnp.where(ok, tile, NEG_INF)
    neg = jnp.full((GRID_W, LANES), NEG_INF, F32)
    for var, shift in enumerate((1, 0, -1)):
        for i in range(WIN_TILES):
            cp = i + shift
            for rq in range(ROWS_PER_TILE):
                for p in range(MXU_N // LANES):
                    if 0 <= cp < WIN_TILES:
                        jr = ROWS_PER_TILE * cp + 2 * p
                        inside = [shift != 0 or 0 <= j - rq < WIN_R for j in (jr, jr + 1)]
                        tile = v_ref[jr - rq + (ROWS_PER_TILE - 1)]
                    else:
                        inside = [False, False]
                    if not any(inside):
                        tile = neg
                    elif not all(inside):
                        tile = jnp.where(hi, tile if inside[1] else neg, tile if inside[0] else neg)
                    e_ref[var, 0, i, rq * GRID_W:(rq + 1) * GRID_W, p * LANES:(p + 1) * LANES] = tile


def _bias_tiles(table_flat, layer):
    shape = (N_BAND_VARIANTS, N_HEADS, WIN_TILES, TILE, MXU_N)
    return pl.pallas_call(
        functools.partial(_bias_kernel, layer=layer),
        grid=(N_HEADS,),
        in_specs=[pl.BlockSpec(memory_space=pltpu.SMEM)],
        out_specs=pl.BlockSpec((N_BAND_VARIANTS, 1, WIN_TILES, TILE, MXU_N), lambda h: (0, h, 0, 0, 0)),
        out_shape=jax.ShapeDtypeStruct(shape, F32),
        scratch_shapes=[pltpu.VMEM((N_BIAS_V, GRID_W, LANES), F32)],
        name="rel_bias_tiles",
    )(table_flat)


def _proj_kernel(*refs, n_state_in, all_layers):
    x_ref, mod_ref, g_ref, w_ref, b_ref = refs[:5]
    q_ref, k_ref, v_ref, u_ref, h_ref, *state_refs = refs[5 + n_state_in:]
    h = _modulated(x_ref[0], mod_ref[0], g_ref[...]).astype(BF16)
    h_ref[0] = h
    bias = jnp.concatenate([b_ref[:, COL_GLU:COL_ZB], b_ref[:, :COL_ZA]], axis=1)
    p = jnp.dot(h, w_ref[...], preferred_element_type=F32) + bias
    a = p[:, :W_B]
    gate = p[:, W_B:2 * W_B]
    u_ref[0] = a * jax.nn.sigmoid(gate)
    q = (p[:, 2 * W_B:2 * W_B + W_A] * (HEAD_DIM ** -0.5)).astype(BF16)
    k = p[:, 2 * W_B + W_A:2 * W_B + 2 * W_A]
    v = p[:, 2 * W_B + 2 * W_A:]
    k16 = k.astype(BF16)
    v16 = v.astype(BF16)
    for blk in range(N_BLK):
        cols = slice(blk * MXU_N, (blk + 1) * MXU_N)
        q_ref[0, blk] = q[:, cols]
        k_ref[0, blk] = k16[:, cols]
        v_ref[0, blk] = v16[:, cols]
    if state_refs:
        for d in (range(DEPTH) if all_layers else range(1)):
            state_refs[0][0, d] = k
            state_refs[1][0, d] = v


def _layer_spec(shape, layer):
    return pl.BlockSpec((None,) + tuple(shape), lambda i, t: (layer,) + (0,) * len(shape))


def _weight_spec(shape, layer):
    return pl.BlockSpec((None,) + tuple(shape), lambda i, t: (layer,) + (0,) * len(shape),
                        pipeline_mode=pl.Buffered(1))


def _proj(x, mods, mod_row, layer, rms_g, w_p, b_in, state):
    bsz, seq, _ = x.shape
    tile = min(PROJ_TILE, seq)
    tok = lambda i, t: (i, t, 0)
    blk_tok = lambda i, t: (i, 0, t, 0)
    out_shape = ([jax.ShapeDtypeStruct((bsz, N_BLK, seq, MXU_N), BF16)] * 3
                 + [jax.ShapeDtypeStruct((bsz, seq, W_B), F32), jax.ShapeDtypeStruct((bsz, seq, D_MODEL), BF16)])
    out_specs = ([pl.BlockSpec((1, N_BLK, tile, MXU_N), blk_tok)] * 3
                 + [pl.BlockSpec((1, tile, W_B), tok), pl.BlockSpec((1, tile, D_MODEL), tok)])
    n_out = len(out_specs)
    in_specs = [
        pl.BlockSpec((1, tile, D_MODEL), tok),
        pl.BlockSpec((None, 1, 1, 3 * D_MODEL), lambda i, t: (layer, mod_row(i), 0, 0)),
        _layer_spec((1, D_MODEL), layer),
        _weight_spec((D_MODEL, N_QKVU), layer),
        _layer_spec((1, D_IN), layer),
    ]
    args = [x, mods, rms_g, w_p, b_in]
    aliases = {}
    if state is not None:
        out_shape += [jax.ShapeDtypeStruct((bsz, DEPTH, seq, W_A), F32)] * 2
        if state:
            in_specs += [pl.BlockSpec(memory_space=pl.ANY)] * 2
            aliases = {len(args): n_out, len(args) + 1: n_out + 1}
            args += list(state)
            out_specs += [pl.BlockSpec((1, 1, tile, W_A), lambda i, t: (i, layer, t, 0))] * 2
        else:
            out_specs += [pl.BlockSpec((1, DEPTH, tile, W_A), lambda i, t: (i, 0, t, 0))] * 2
    return pl.pallas_call(
        functools.partial(_proj_kernel, n_state_in=len(state or ()), all_layers=(state == ())),
        grid=(bsz, seq // tile),
        in_specs=in_specs,
        out_specs=out_specs,
        out_shape=out_shape,
        input_output_aliases=aliases,
        compiler_params=pltpu.CompilerParams(
            dimension_semantics=("arbitrary", "arbitrary"), vmem_limit_bytes=VMEM_LIMIT),
        name="proj_qkvu",
    )(*args)


def _softmax_pv(s_parts, v_parts):
    s = jnp.concatenate(s_parts, axis=1) if len(s_parts) > 1 else s_parts[0]
    m = jnp.max(s, axis=1, keepdims=True)
    p = jnp.exp(s - m)
    l = jnp.sum(p, axis=1, keepdims=True)
    p = p.astype(BF16)
    o = None
    start = 0
    for vp in v_parts:
        n = vp.shape[0]
        part = jnp.dot(p[:, start:start + n], vp, preferred_element_type=F32)
        o = part if o is None else o + part
        start += n
    return o / l


def _conv_ln_and_zg(u_ref, ubuf, dww_ref, dwb_ref, lng_ref, lnb_ref, h_ref, wz_ref, bias, t, n_tiles):
    if n_tiles == 1:
        ubuf[0:HALO, :] = jnp.zeros((HALO, W_B), F32)
        ubuf[HALO:HALO + TILE, :] = u_ref[0]
        ubuf[HALO + TILE:, :] = jnp.zeros((HALO, W_B), F32)
    else:
        t0 = pl.multiple_of(t * TILE, TILE)
        left = u_ref[0, pl.ds(pl.multiple_of(jnp.maximum(t0 - HALO, 0), HALO), HALO), :]
        right = u_ref[0, pl.ds(pl.multiple_of(jnp.minimum(t0 + TILE, n_tiles * TILE - HALO), HALO), HALO), :]
        ubuf[0:HALO, :] = jnp.where(t > 0, left, 0.0)
        ubuf[HALO:HALO + TILE, :] = u_ref[0, pl.ds(t0, TILE), :]
        ubuf[HALO + TILE:, :] = jnp.where(t < n_tiles - 1, right, 0.0)
    half = TILE // CONV_ROW_CHUNKS
    span = half + 4 * SUBLANES
    n_units = (W_B // LANES) * CONV_ROW_CHUNKS
    n_ztiles = N_ZG // MXU_N
    h16 = h_ref[0]
    units, ztiles = [], []

    def conv_unit(k):
        c, hf = divmod(k, CONV_ROW_CHUNKS)
        lanes = slice(c * LANES, (c + 1) * LANES)
        acc = jnp.zeros((half, LANES), F32)
        dep = (k * n_ztiles) // n_units - 3
        if dep >= 0:
            acc = acc + _held(ztiles[dep][0:half, 0:LANES])
        base = ubuf[hf * half:hf * half + span, lanes]
        for rr in range(SUBLANES):
            sh = base if rr == 0 else pltpu.roll(base, span - rr, axis=0)
            for j in range(4):
                kk = SUBLANES * j + rr - (HALO - CONV_PAD)
                if 0 <= kk < CONV_K:
                    acc = acc + dww_ref[kk:kk + 1, lanes] * sh[SUBLANES * j:SUBLANES * j + half]
        units.append(acc)

    def proj_tile(j):
        cols = slice(j * MXU_N, (j + 1) * MXU_N)
        lhs = h16
        dep = (j * n_units) // n_ztiles - 1
        if dep >= 0:
            z0 = _held(units[dep][0:2 * SUBLANES, :]).astype(BF16)
            top = h16[0:2 * SUBLANES] + jnp.concatenate([z0] * (D_MODEL // LANES), axis=1)
            lhs = jnp.concatenate([top, h16[2 * SUBLANES:]], axis=0)
        ztiles.append(jnp.dot(lhs, wz_ref[:, cols], preferred_element_type=F32) + bias[:, cols])

    events = sorted([(k * n_ztiles, 0, k) for k in range(n_units)] + [(j * n_units, 1, j) for j in range(n_ztiles)])
    for _, is_tile, idx in events:
        (proj_tile if is_tile else conv_unit)(idx)
    cv = jnp.concatenate(
        [jnp.concatenate(units[c * CONV_ROW_CHUNKS:(c + 1) * CONV_ROW_CHUNKS], axis=0) for c in range(W_B // LANES)],
        axis=1) + dwb_ref[...]
    mu = jnp.mean(cv, axis=-1, keepdims=True)
    var = jnp.mean(jnp.square(cv - mu), axis=-1, keepdims=True)
    ln = (cv - mu) * lax.rsqrt(var + EPS) * lng_ref[...] + lnb_ref[...]
    return ln, jnp.concatenate(ztiles, axis=1)


def _held(x):
    bits = pltpu.bitcast(x, jnp.uint32)
    bits = lax.shift_right_logical(lax.shift_right_logical(bits, jnp.uint32(16)), jnp.uint32(16))
    return pltpu.bitcast(bits, F32)


def _mix_kernel(*refs, latent, final, n_tiles):
    if latent:
        (x_ref, mod_ref, h_ref, wz_ref, b_ref, q_ref, k_ref, v_ref, u_ref, ck_ref, cv_ref, e_ref,
         dww_ref, dwb_ref, lng_ref, lnb_ref, wpa_ref, wpb_ref, wo_ref, fg_ref, y_ref, ubuf) = refs
    else:
        (x_ref, mod_ref, h_ref, wz_ref, b_ref, q_ref, k_ref, v_ref, u_ref,
         dww_ref, dwb_ref, lng_ref, lnb_ref, wpa_ref, wpb_ref, wo_ref, fg_ref, y_ref, ubuf) = refs
    t = pl.program_id(1)
    x = x_ref[0]
    mod = mod_ref[0]
    bias = jnp.concatenate([b_ref[:, COL_ZA:COL_GLU], b_ref[:, COL_ZB:]], axis=1)
    ln, zg = _conv_ln_and_zg(u_ref, ubuf, dww_ref, dwb_ref, lng_ref, lnb_ref, h_ref, wz_ref, bias, t, n_tiles)
    z_a = zg[:, :W_A]
    z_b = zg[:, W_A:W_A + W_B]
    g_a = jax.nn.sigmoid(zg[:, W_A + W_B:W_A + W_B + D_MODEL])
    g_b = jax.nn.sigmoid(zg[:, W_A + W_B + D_MODEL:])

    lane_row = lax.broadcasted_iota(jnp.int32, (1, MXU_N), 1)
    lane_full = lax.broadcasted_iota(jnp.int32, (TILE, MXU_N), 1)
    if latent:
        first_row = ROWS_PER_TILE * t
        win_row = jnp.clip(first_row - ROWS_PER_TILE, 0, (n_tiles - WIN_TILES) * ROWS_PER_TILE)
        win_tok = pl.multiple_of(win_row * GRID_W, TILE)
    blocks = []
    for blk in range(N_BLK):
        cols = slice(blk * MXU_N, (blk + 1) * MXU_N)
        qb = q_ref[0, blk]
        if latent:
            kw = k_ref[0, blk, pl.ds(win_tok, N_WIN), :]
            vw = v_ref[0, blk, pl.ds(win_tok, N_WIN), :]
            ckb = ck_ref[0, :, cols].astype(BF16)
            cvb = cv_ref[0, :, cols].astype(BF16)
        else:
            kw = k_ref[0, blk]
            vw = v_ref[0, blk]
        acc = jnp.zeros((TILE, MXU_N), F32)
        for hh in range(HEADS_PER_BLK):
            head = blk * HEADS_PER_BLK + hh
            lo = hh * HEAD_DIM
            hm = jnp.where((lane_row >= lo) & (lane_row < lo + HEAD_DIM), 1.0, 0.0).astype(BF16)
            qh = qb * hm
            s = _dot_t(qh, kw)
            if latent:
                parts = []
                for i in range(WIN_TILES):
                    parts.append(s[:, i * MXU_N:(i + 1) * MXU_N] + e_ref[head, i])
                parts.append(_dot_t(qh, ckb))
                o = _softmax_pv(parts, [vw, cvb])
            else:
                o = _softmax_pv([s], [vw])
            acc = jnp.where((lane_full >= lo) & (lane_full < lo + HEAD_DIM), o, acc)
        blocks.append(acc)
    attn = jnp.concatenate(blocks, axis=1)
    y_a = (attn * jax.nn.silu(z_a)).astype(BF16)

    y_b =(jax.nn.silu(ln) * jax.nn.silu(z_b)).astype(BF16)

    m = (g_a * jnp.dot(y_a, wpa_ref[...], preferred_element_type=F32)
         + g_b * jnp.dot(y_b, wpb_ref[...], preferred_element_type=F32))
    o = jnp.dot(m.astype(BF16), wo_ref[...], preferred_element_type=F32)
    y = x + mod[:, 2 * D_MODEL:] * o
    if final:
        y = _rmsnorm(y, fg_ref[...])
    y_ref[0] = y


def _mix(x, mods, mod_row, layer, params, q, k, v, u, h, ctx_kv, e_tiles, final):
    _, w_z, b_in, dw_w, dw_b, ln_g, ln_b, wpa, wpb, wo, fg = params
    bsz, seq, _ = x.shape
    nt = seq // TILE
    latent = ctx_kv is not None
    tok = lambda i, t: (i, t, 0)
    in_specs = [
        pl.BlockSpec((1, TILE, D_MODEL), tok),
        pl.BlockSpec((None, 1, 1, 3 * D_MODEL), lambda i, t: (layer, mod_row(i), 0, 0)),
        pl.BlockSpec((1, TILE, D_MODEL), tok),
        _weight_spec((D_MODEL, N_ZG), layer),
        _layer_spec((1, D_IN), layer),
        pl.BlockSpec((1, N_BLK, TILE, MXU_N), lambda i, t: (i, 0, t, 0)),
        pl.BlockSpec((1, N_BLK, seq, MXU_N), lambda i, t: (i, 0, 0, 0)),
        pl.BlockSpec((1, N_BLK, seq, MXU_N), lambda i, t: (i, 0, 0, 0)),
        pl.BlockSpec((1, seq, W_B), lambda i, t: (i, 0, 0)),
    ]
    args = [x, mods, h, w_z, b_in, q, k, v, u]
    if latent:
        ck, cv = ctx_kv
        past = ck.shape[2]
        ctx_spec = pl.BlockSpec((1, None, past, W_A), lambda i, t: (i, layer, 0, 0))
        assert nt >= WIN_TILES
        variant = lambda t: jnp.where(t == 0, 0, jnp.where(t == nt - 1, N_BAND_VARIANTS - 1, 1))
        in_specs += [ctx_spec, ctx_spec,
                     pl.BlockSpec((None, N_HEADS, WIN_TILES, TILE, MXU_N), lambda i, t: (variant(t), 0, 0, 0, 0))]
        args += [ck, cv, e_tiles]
    in_specs += [
        _layer_spec((CONV_K, W_B), layer),
        _layer_spec((1, W_B), layer),
        _layer_spec((1, W_B), layer),
        _layer_spec((1, W_B), layer),
        _weight_spec((W_A, D_MODEL), layer),
        _weight_spec((W_B, D_MODEL), layer),
        _weight_spec((D_MODEL, D_MODEL), layer),
        pl.BlockSpec((1, D_MODEL), lambda i, t: (0, 0)),
    ]
    args += [dw_w, dw_b, ln_g, ln_b, wpa, wpb, wo, fg]
    return pl.pallas_call(
        functools.partial(_mix_kernel, latent=latent, final=final, n_tiles=nt),
        grid=(bsz, nt),
        in_specs=in_specs,
        out_specs=pl.BlockSpec((1, TILE, D_MODEL), tok),
        out_shape=jax.ShapeDtypeStruct((bsz, seq, D_MODEL), F32),
        scratch_shapes=[pltpu.VMEM((TILE + 2 * HALO, W_B), F32)],
        compiler_params=pltpu.CompilerParams(
            dimension_semantics=("arbitrary", "arbitrary"), vmem_limit_bytes=VMEM_LIMIT),
        name="mix_latent" if latent else "mix_context",
    )(*args)


def kernel(x_prompt, x_sample, cache_k, cache_v, c, c_ctx, rms_g, w_ada, b_ada, w_in, b_in, rel_bias,
           dw_w, dw_b, ln_g, ln_b, w_proj_a, w_proj_b, w_out, final_g):
    dec_batch = x_sample.shape[0]
    ctx_row = dec_batch
    cstack = jnp.concatenate([c, c_ctx[None, :], jnp.zeros((MOD_ROWS - dec_batch - 1, D_MODEL), F32)], axis=0)
    mods = _mods(cstack, w_ada, b_ada).reshape(DEPTH, MOD_ROWS, 1, 3 * D_MODEL)

    glu_blk = COL_GLU // CAST_TN
    za_blk = COL_ZA // CAST_TN
    zb_blk = COL_ZB // CAST_TN
    n_glu = 2 * W_B // CAST_TN
    w_p = _cast_cols(w_in, N_QKVU, lambda j: jnp.where(j < n_glu, j + glu_blk, j - n_glu), "cast_w_qkvu")
    w_z = _cast_cols(w_in, N_ZG, lambda j: jnp.where(j == 0, za_blk, j + (zb_blk - 1)), "cast_w_zg")
    wpa, wpb, wo = _cast_many([w_proj_a, w_proj_b, w_out], "cast_w_merge")

    row = lambda a: a.reshape(DEPTH, 1, a.shape[-1])
    params = (row(rms_g), w_z, row(b_in), dw_w, row(dw_b), row(ln_g), row(ln_b), wpa, wpb, wo, final_g[None, :])
    past = cache_k.shape[2]
    ck = cache_k.reshape(dec_batch, DEPTH, past, W_A)
    cv = cache_v.reshape(dec_batch, DEPTH, past, W_A)
    table_flat = rel_bias.reshape(-1)

    x = x_prompt
    state = ()
    ctx_mod = lambda i: ctx_row
    for l in range(DEPTH):
        q, k, v, u, h, *state = _proj(x, mods, ctx_mod, l, params[0], w_p, params[2], tuple(state))
        x = _mix(x, mods, ctx_mod, l, params, q, k, v, u, h, None, None, final=(l == DEPTH - 1))
    bsz, seq, _ = x_prompt.shape
    state_k = state[0].reshape(bsz, DEPTH, seq, N_HEADS, HEAD_DIM)
    state_v = state[1].reshape(bsz, DEPTH, seq, N_HEADS, HEAD_DIM)

    z = x_sample
    lat_mod = lambda i: i
    for l in range(DEPTH):
        e_tiles = _bias_tiles(table_flat, l)
        q, k, v, u, h = _proj(z, mods, lat_mod, l, params[0], w_p, params[2], None)
        z = _mix(z, mods, lat_mod, l, params, q, k, v, u, h, (ck, cv), e_tiles, final=(l == DEPTH - 1))
    return (x, z, state_k, state_v)
```

```python
import functools

import jax
import jax.numpy as jnp
from jax import lax
from jax.experimental import pallas as pl
from jax.experimental.pallas import tpu as pltpu

F32 = jnp.float32
BF16 = jnp.bfloat16

D_MODEL = 1024
N_HEADS = 8
HEAD_DIM = 64
W_A = N_HEADS * HEAD_DIM
W_B = 512
CONV_K = 31
CONV_PAD = CONV_K // 2
GRID_W = 64
WIN_R = 8
WIN_C = 16
EPS = 1e-6
DEPTH = 2
N_DR = 2 * WIN_R - 1
N_DC = 2 * WIN_C - 1

COL_ZA = 3 * W_A
COL_GLU = COL_ZA + W_A
COL_ZB = COL_GLU + 2 * W_B
D_IN = COL_ZB + W_B + 2 * D_MODEL
N_QKVU = 3 * W_A + 2 * W_B
N_ZG = W_A + W_B + 2 * D_MODEL

LANES = 128
SUBLANES = 8
MXU_N = 256

ROWS_PER_TILE = 4
TILE = ROWS_PER_TILE * GRID_W
PROJ_TILE = 512
WIN_TILES = 3
N_WIN = WIN_TILES * TILE
HALO = 2 * SUBLANES
CONV_ROW_CHUNKS = 4
HEADS_PER_BLK = MXU_N // HEAD_DIM
N_BLK = N_HEADS // HEADS_PER_BLK
N_BIAS_V = ROWS_PER_TILE * WIN_TILES + ROWS_PER_TILE - 2
N_BAND_VARIANTS = 3
MOD_ROWS = 8
MOD_TN = 1024
CAST_TN = 512
VMEM_LIMIT = 58 * 1024 * 1024
NEG_INF = float("-inf")


def _rmsnorm(x, g):
    return x * lax.rsqrt(jnp.mean(x * x, axis=-1, keepdims=True) + EPS) * g


def _modulated(x, mod, g):
    shift = mod[:, :D_MODEL]
    scale = mod[:, D_MODEL:2 * D_MODEL]
    return _rmsnorm(x, g) * (1.0 + scale) + shift


def _sigmoid(x):
    return 0.5 * jnp.tanh(0.5 * x) + 0.5


def _silu(x):
    half = 0.5 * x
    return half * jnp.tanh(half) + half


def _dot_t(a, b):
    return lax.dot_general(a, b, (((1,), (1,)), ((), ())), preferred_element_type=F32)


def _mod_kernel(c_ref, w_ref, b_ref, o_ref):
    s = jax.nn.silu(c_ref[...])
    o_ref[0] = jnp.dot(s.astype(BF16), w_ref[0].astype(BF16), preferred_element_type=F32) + b_ref[0]


def _mods(cstack, w_ada, b_ada):
    n = 3 * D_MODEL
    return pl.pallas_call(
        _mod_kernel,
        grid=(DEPTH, n // MOD_TN),
        in_specs=[
            pl.BlockSpec((MOD_ROWS, D_MODEL), lambda l, j: (0, 0)),
            pl.BlockSpec((1, D_MODEL, MOD_TN), lambda l, j: (l, 0, j)),
            pl.BlockSpec((1, 1, MOD_TN), lambda l, j: (l, 0, j)),
        ],
        out_specs=pl.BlockSpec((1, MOD_ROWS, MOD_TN), lambda l, j: (l, 0, j)),
        out_shape=jax.ShapeDtypeStruct((DEPTH, MOD_ROWS, n), F32),
        name="adaln_mods",
    )(cstack, w_ada, b_ada.reshape(DEPTH, 1, n))


def _cast_kernel(w_ref, o_ref):
    o_ref[...] = w_ref[...].astype(BF16)


def _cast_cols(w, n_out, src_block, name):
    depth, kdim, _ = w.shape
    return pl.pallas_call(
        _cast_kernel,
        grid=(depth, n_out // CAST_TN),
        in_specs=[pl.BlockSpec((1, kdim, CAST_TN), lambda l, j: (l, 0, src_block(j)))],
        out_specs=pl.BlockSpec((1, kdim, CAST_TN), lambda l, j: (l, 0, j)),
        out_shape=jax.ShapeDtypeStruct((depth, kdim, n_out), BF16),
        name=name,
    )(w)


def _cast_many_kernel(*refs):
    n = len(refs) // 2
    for w_ref, o_ref in zip(refs[:n], refs[n:]):
        o_ref[...] = w_ref[...].astype(BF16)


def _cast_many(ws, name):
    depth, _, n = ws[0].shape
    spec = lambda w: pl.BlockSpec((1, w.shape[1], CAST_TN), lambda l, j: (l, 0, j))
    return pl.pallas_call(
        _cast_many_kernel,
        grid=(depth, n // CAST_TN),
        in_specs=[spec(w) for w in ws],
        out_specs=[spec(w) for w in ws],
        out_shape=[jax.ShapeDtypeStruct(w.shape, BF16) for w in ws],
        name=name,
    )(*ws)


def _bias_kernel(t_ref, e_ref, v_ref, *, layer):
    h = pl.program_id(0)
    lane = lax.broadcasted_iota(jnp.int32, (GRID_W, LANES), 1)
    cq = lax.broadcasted_iota(jnp.int32, (GRID_W, LANES), 0)
    ck = lane & (GRID_W - 1)
    hi = lane >= GRID_W
    c0 = jnp.clip(cq - WIN_C // 2, 0, GRID_W - WIN_C)
    ok = (ck >= c0) & (ck < c0 + WIN_C)

    first = (layer * N_HEADS + h) * N_DR
    rows = [jnp.broadcast_to(t_ref[pl.ds(first + d, 1), :], (GRID_W, LANES)) for d in range(N_BIAS_V + 1)]

    def toeplitz(d, lane0):
        return pltpu.roll(rows[d], (lane0 - (WIN_C - 1)) % LANES, axis=1, stride=1, stride_axis=0)

    for d in range(N_BIAS_V):
        tile = jnp.where(hi, toeplitz(d + 1, GRID_W), toeplitz(d, 0))
        v_ref[d] = jnp.where(ok, tile, NEG_INF)
    neg = jnp.full((GRID_W, LANES), NEG_INF, F32)
    for var, shift in enumerate((1, 0, -1)):
        for i in range(WIN_TILES):
            cp = i + shift
            for rq in range(ROWS_PER_TILE):
                for p in range(MXU_N // LANES):
                    if 0 <= cp < WIN_TILES:
                        jr = ROWS_PER_TILE * cp + 2 * p
                        inside = [shift != 0 or 0 <= j - rq < WIN_R for j in (jr, jr + 1)]
                        tile = v_ref[jr - rq + (ROWS_PER_TILE - 1)]
                    else:
                        inside = [False, False]
                    if not any(inside):
                        tile = neg
                    elif not all(inside):
                        tile = jnp.where(hi, tile if inside[1] else neg, tile if inside[0] else neg)
                    e_ref[var, 0, i, rq * GRID_W:(rq + 1) * GRID_W, p * LANES:(p + 1) * LANES] = tile


def _bias_tiles(table_flat, layer):
    shape = (N_BAND_VARIANTS, N_HEADS, WIN_TILES, TILE, MXU_N)
    return pl.pallas_call(
        functools.partial(_bias_kernel, layer=layer),
        grid=(N_HEADS,),
        in_specs=[pl.BlockSpec(table_flat.shape, lambda h: (0, 0))],
        out_specs=pl.BlockSpec((N_BAND_VARIANTS, 1, WIN_TILES, TILE, MXU_N), lambda h: (0, h, 0, 0, 0)),
        out_shape=jax.ShapeDtypeStruct(shape, F32),
        scratch_shapes=[pltpu.VMEM((N_BIAS_V, GRID_W, LANES), F32)],
        name="rel_bias_tiles",
    )(table_flat)


def _proj_kernel(*refs, n_state_in, all_layers):
    x_ref, mod_ref, g_ref, w_ref, b_ref = refs[:5]
    q_ref, k_ref, v_ref, u_ref, h_ref, *state_refs = refs[5 + n_state_in:]
    h = _modulated(x_ref[0], mod_ref[0], g_ref[...]).astype(BF16)
    h_ref[0] = h
    bias = jnp.concatenate([b_ref[:, COL_GLU:COL_ZB], b_ref[:, :COL_ZA]], axis=1)
    p = jnp.dot(h, w_ref[...], preferred_element_type=F32) + bias
    a = p[:, :W_B]
    gate = p[:, W_B:2 * W_B]
    u_ref[0] = a * jax.nn.sigmoid(gate)
    q = (p[:, 2 * W_B:2 * W_B + W_A] * (HEAD_DIM ** -0.5)).astype(BF16)
    k = p[:, 2 * W_B + W_A:2 * W_B + 2 * W_A]
    v = p[:, 2 * W_B + 2 * W_A:]
    k16 = k.astype(BF16)
    v16 = v.astype(BF16)
    for blk in range(N_BLK):
        cols = slice(blk * MXU_N, (blk + 1) * MXU_N)
        q_ref[0, blk] = q[:, cols]
        k_ref[0, blk] = k16[:, cols]
        v_ref[0, blk] = v16[:, cols]
    if state_refs:
        for d in (range(DEPTH) if all_layers else range(1)):
            state_refs[0][0, d] = k
            state_refs[1][0, d] = v


def _layer_spec(shape, layer):
    return pl.BlockSpec((None,) + tuple(shape), lambda i, t: (layer,) + (0,) * len(shape))


def _weight_spec(shape, layer):
    return pl.BlockSpec((None,) + tuple(shape), lambda i, t: (layer,) + (0,) * len(shape),
                        pipeline_mode=pl.Buffered(1))


def _proj(x, mods, mod_row, layer, rms_g, w_p, b_in, state):
    bsz, seq, _ = x.shape
    tile = min(PROJ_TILE, seq)
    tok = lambda i, t: (i, t, 0)
    blk_tok = lambda i, t: (i, 0, t, 0)
    out_shape = ([jax.ShapeDtypeStruct((bsz, N_BLK, seq, MXU_N), BF16)] * 3
                 + [jax.ShapeDtypeStruct((bsz, seq, W_B), F32), jax.ShapeDtypeStruct((bsz, seq, D_MODEL), BF16)])
    out_specs = ([pl.BlockSpec((1, N_BLK, tile, MXU_N), blk_tok)] * 3
                 + [pl.BlockSpec((1, tile, W_B), tok), pl.BlockSpec((1, tile, D_MODEL), tok)])
    n_out = len(out_specs)
    in_specs = [
        pl.BlockSpec((1, tile, D_MODEL), tok),
        pl.BlockSpec((None, 1, 1, 3 * D_MODEL), lambda i, t: (layer, mod_row(i), 0, 0)),
        _layer_spec((1, D_MODEL), layer),
        _weight_spec((D_MODEL, N_QKVU), layer),
        _layer_spec((1, D_IN), layer),
    ]
    args = [x, mods, rms_g, w_p, b_in]
    aliases = {}
    if state is not None:
        out_shape += [jax.ShapeDtypeStruct((bsz, DEPTH, seq, W_A), F32)] * 2
        if state:
            in_specs += [pl.BlockSpec(memory_space=pl.ANY)] * 2
            aliases = {len(args): n_out, len(args) + 1: n_out + 1}
            args += list(state)
            out_specs += [pl.BlockSpec((1, 1, tile, W_A), lambda i, t: (i, layer, t, 0))] * 2
        else:
            out_specs += [pl.BlockSpec((1, DEPTH, tile, W_A), lambda i, t: (i, 0, t, 0))] * 2
    return pl.pallas_call(
        functools.partial(_proj_kernel, n_state_in=len(state or ()), all_layers=(state == ())),
        grid=(bsz, seq // tile),
        in_specs=in_specs,
        out_specs=out_specs,
        out_shape=out_shape,
        input_output_aliases=aliases,
        compiler_params=pltpu.CompilerParams(
            dimension_semantics=("arbitrary", "arbitrary"), vmem_limit_bytes=VMEM_LIMIT),
        name="proj_qkvu",
    )(*args)


def _softmax_pv(s_parts, v_parts):
    s = jnp.concatenate(s_parts, axis=1) if len(s_parts) > 1 else s_parts[0]
    m = jnp.max(s, axis=1, keepdims=True)
    p = jnp.exp(s - m)
    l = jnp.sum(p, axis=1, keepdims=True)
    p = p.astype(BF16)
    o = None
    start = 0
    for vp in v_parts:
        n = vp.shape[0]
        part = jnp.dot(p[:, start:start + n], vp, preferred_element_type=F32)
        o = part if o is None else o + part
        start += n
    return o / l


def _conv_ln_and_zg(u_ref, ubuf, dww_ref, dwb_ref, lng_ref, lnb_ref, h_ref, wz_ref, bias, t, n_tiles):
    if n_tiles == 1:
        ubuf[0:HALO, :] = jnp.zeros((HALO, W_B), F32)
        ubuf[HALO:HALO + TILE, :] = u_ref[0]
        ubuf[HALO + TILE:, :] = jnp.zeros((HALO, W_B), F32)
    else:
        t0 = pl.multiple_of(t * TILE, TILE)
        left = u_ref[0, pl.ds(pl.multiple_of(jnp.maximum(t0 - HALO, 0), HALO), HALO), :]
        right = u_ref[0, pl.ds(pl.multiple_of(jnp.minimum(t0 + TILE, n_tiles * TILE - HALO), HALO), HALO), :]
        ubuf[0:HALO, :] = jnp.where(t > 0, left, 0.0)
        ubuf[HALO:HALO + TILE, :] = u_ref[0, pl.ds(t0, TILE), :]
        ubuf[HALO + TILE:, :] = jnp.where(t < n_tiles - 1, right, 0.0)
    half = TILE // CONV_ROW_CHUNKS
    span = half + 4 * SUBLANES
    n_units = (W_B // LANES) * CONV_ROW_CHUNKS
    n_ztiles = N_ZG // MXU_N
    h16 = h_ref[0]
    units, ztiles = [], []

    def conv_unit(k):
        c, hf = divmod(k, CONV_ROW_CHUNKS)
        lanes = slice(c * LANES, (c + 1) * LANES)
        acc = jnp.zeros((half, LANES), F32)
        dep = (k * n_ztiles) // n_units - 3
        if dep >= 0:
            acc = acc + _held(ztiles[dep][0:half, 0:LANES])
        base = ubuf[hf * half:hf * half + span, lanes]
        for rr in range(SUBLANES):
            sh = base if rr == 0 else pltpu.roll(base, span - rr, axis=0)
            for j in range(4):
                kk = SUBLANES * j + rr - (HALO - CONV_PAD)
                if 0 <= kk < CONV_K:
                    acc = acc + dww_ref[kk:kk + 1, lanes] * sh[SUBLANES * j:SUBLANES * j + half]
        units.append(acc)

    def proj_tile(j):
        cols = slice(j * MXU_N, (j + 1) * MXU_N)
        lhs = h16
        dep = (j * n_units) // n_ztiles - 1
        if dep >= 0:
            z0 = _held(units[dep][0:2 * SUBLANES, :]).astype(BF16)
            top = h16[0:2 * SUBLANES] + jnp.concatenate([z0] * (D_MODEL // LANES), axis=1)
            lhs = jnp.concatenate([top, h16[2 * SUBLANES:]], axis=0)
        ztiles.append(jnp.dot(lhs, wz_ref[:, cols], preferred_element_type=F32) + bias[:, cols])

    events = sorted([(k * n_ztiles, 0, k) for k in range(n_units)] + [(j * n_units, 1, j) for j in range(n_ztiles)])
    for _, is_tile, idx in events:
        (proj_tile if is_tile else conv_unit)(idx)
    cv = jnp.concatenate(
        [jnp.concatenate(units[c * CONV_ROW_CHUNKS:(c + 1) * CONV_ROW_CHUNKS], axis=0) for c in range(W_B // LANES)],
        axis=1) + dwb_ref[...]
    mu = jnp.mean(cv, axis=-1, keepdims=True)
    var = jnp.mean(jnp.square(cv - mu), axis=-1, keepdims=True)
    ln = (cv - mu) * lax.rsqrt(var + EPS) * lng_ref[...] + lnb_ref[...]
    return ln, jnp.concatenate(ztiles, axis=1)


def _held(x):
    bits = pltpu.bitcast(x, jnp.uint32)
    bits = lax.shift_right_logical(lax.shift_right_logical(bits, jnp.uint32(16)), jnp.uint32(16))
    return pltpu.bitcast(bits, F32)


def _mix_kernel(*refs, latent, final, n_tiles):
    if latent:
        (x_ref, mod_ref, h_ref, wz_ref, b_ref, q_ref, k_ref, v_ref, u_ref, ck_ref, cv_ref, e_ref,
         dww_ref, dwb_ref, lng_ref, lnb_ref, wpa_ref, wpb_ref, wo_ref, fg_ref, y_ref, ubuf) = refs
    else:
        (x_ref, mod_ref, h_ref, wz_ref, b_ref, q_ref, k_ref, v_ref, u_ref,
         dww_ref, dwb_ref, lng_ref, lnb_ref, wpa_ref, wpb_ref, wo_ref, fg_ref, y_ref, ubuf) = refs
    t = pl.program_id(1)
    x = x_ref[0]
    mod = mod_ref[0]
    bias = jnp.concatenate([b_ref[:, COL_ZA:COL_GLU], b_ref[:, COL_ZB:]], axis=1)
    ln, zg = _conv_ln_and_zg(u_ref, ubuf, dww_ref, dwb_ref, lng_ref, lnb_ref, h_ref, wz_ref, bias, t, n_tiles)
    z_a = zg[:, :W_A]
    z_b = zg[:, W_A:W_A + W_B]
    g_a = _sigmoid(zg[:, W_A + W_B:W_A + W_B + D_MODEL])
    g_b = _sigmoid(zg[:, W_A + W_B + D_MODEL:])

    lane_row = lax.broadcasted_iota(jnp.int32, (1, MXU_N), 1)
    lane_full = lax.broadcasted_iota(jnp.int32, (TILE, MXU_N), 1)
    if latent:
        first_row = ROWS_PER_TILE * t
        win_row = jnp.clip(first_row - ROWS_PER_TILE, 0, (n_tiles - WIN_TILES) * ROWS_PER_TILE)
        win_tok = pl.multiple_of(win_row * GRID_W, TILE)
    blocks = []
    for blk in range(N_BLK):
        cols = slice(blk * MXU_N, (blk + 1) * MXU_N)
        qb = q_ref[0, blk]
        if latent:
            kw = k_ref[0, blk, pl.ds(win_tok, N_WIN), :]
            vw = v_ref[0, blk, pl.ds(win_tok, N_WIN), :]
            ckb = ck_ref[0, :, cols].astype(BF16)
            cvb = cv_ref[0, :, cols].astype(BF16)
        else:
            kw = k_ref[0, blk]
            vw = v_ref[0, blk]
        acc = jnp.zeros((TILE, MXU_N), F32)
        for hh in range(HEADS_PER_BLK):
            head = blk * HEADS_PER_BLK + hh
            lo = hh * HEAD_DIM
            hm = jnp.where((lane_row >= lo) & (lane_row < lo + HEAD_DIM), 1.0, 0.0).astype(BF16)
            qh = qb * hm
            s = _dot_t(qh, kw)
            if latent:
                parts = []
                for i in range(WIN_TILES):
                    parts.append(s[:, i * MXU_N:(i + 1) * MXU_N] + e_ref[head, i])
                parts.append(_dot_t(qh, ckb))
                o = _softmax_pv(parts, [vw, cvb])
            else:
                o = _softmax_pv([s], [vw])
            acc = jnp.where((lane_full >= lo) & (lane_full < lo + HEAD_DIM), o, acc)
        blocks.append(acc)
    attn = jnp.concatenate(blocks, axis=1)
    y_a = (attn * _silu(z_a)).astype(BF16)
    y_b = (_silu(ln) * _silu(z_b)).astype(BF16)

    m = (g_a * jnp.dot(y_a, wpa_ref[...], preferred_element_type=F32)
         + g_b * jnp.dot(y_b, wpb_ref[...], preferred_element_type=F32))
    o = jnp.dot(m.astype(BF16), wo_ref[...], preferred_element_type=F32)
    y = x + mod[:, 2 * D_MODEL:] * o
    if final:
        y = _rmsnorm(y, fg_ref[...])
    y_ref[0] = y


def _mix(x, mods, mod_row, layer, params, q, k, v, u, h, ctx_kv, e_tiles, final):
    _, w_z, b_in, dw_w, dw_b, ln_g, ln_b, wpa, wpb, wo, fg = params
    bsz, seq, _ = x.shape
    nt = seq // TILE
    latent = ctx_kv is not None
    tok = lambda i, t: (i, t, 0)
    in_specs = [
        pl.BlockSpec((1, TILE, D_MODEL), tok),
        pl.BlockSpec((None, 1, 1, 3 * D_MODEL), lambda i, t: (layer, mod_row(i), 0, 0)),
        pl.BlockSpec((1, TILE, D_MODEL), tok),
        _weight_spec((D_MODEL, N_ZG), layer),
        _layer_spec((1, D_IN), layer),
        pl.BlockSpec((1, N_BLK, TILE, MXU_N), lambda i, t: (i, 0, t, 0)),
        pl.BlockSpec((1, N_BLK, seq, MXU_N), lambda i, t: (i, 0, 0, 0)),
        pl.BlockSpec((1, N_BLK, seq, MXU_N), lambda i, t: (i, 0, 0, 0)),
        pl.BlockSpec((1, seq, W_B), lambda i, t: (i, 0, 0)),
    ]
    args = [x, mods, h, w_z, b_in, q, k, v, u]
    if latent:
        ck, cv = ctx_kv
        past = ck.shape[2]
        ctx_spec = pl.BlockSpec((1, None, past, W_A), lambda i, t: (i, layer, 0, 0))
        assert nt >= WIN_TILES
        variant = lambda t: jnp.where(t == 0, 0, jnp.where(t == nt - 1, N_BAND_VARIANTS - 1, 1))
        in_specs += [ctx_spec, ctx_spec,
                     pl.BlockSpec((None, N_HEADS, WIN_TILES, TILE, MXU_N), lambda i, t: (variant(t), 0, 0, 0, 0))]
        args += [ck, cv, e_tiles]
    in_specs += [
        _layer_spec((CONV_K, W_B), layer),
        _layer_spec((1, W_B), layer),
        _layer_spec((1, W_B), layer),
        _layer_spec((1, W_B), layer),
        _weight_spec((W_A, D_MODEL), layer),
        _weight_spec((W_B, D_MODEL), layer),
        _weight_spec((D_MODEL, D_MODEL), layer),
        pl.BlockSpec((1, D_MODEL), lambda i, t: (0, 0)),
    ]
    args += [dw_w, dw_b, ln_g, ln_b, wpa, wpb, wo, fg]
    return pl.pallas_call(
        functools.partial(_mix_kernel, latent=latent, final=final, n_tiles=nt),
        grid=(bsz, nt),
        in_specs=in_specs,
        out_specs=pl.BlockSpec((1, TILE, D_MODEL), tok),
        out_shape=jax.ShapeDtypeStruct((bsz, seq, D_MODEL), F32),
        scratch_shapes=[pltpu.VMEM((TILE + 2 * HALO, W_B), F32)],
        compiler_params=pltpu.CompilerParams(
            dimension_semantics=("arbitrary", "arbitrary"), vmem_limit_bytes=VMEM_LIMIT),
        name="mix_latent" if latent else "mix_context",
    )(*args)


def kernel(x_prompt, x_sample, cache_k, cache_v, c, c_ctx, rms_g, w_ada, b_ada, w_in, b_in, rel_bias,
           dw_w, dw_b, ln_g, ln_b, w_proj_a, w_proj_b, w_out, final_g):
    dec_batch = x_sample.shape[0]
    ctx_row = dec_batch
    cstack = jnp.concatenate([c, c_ctx[None, :], jnp.zeros((MOD_ROWS - dec_batch - 1, D_MODEL), F32)], axis=0)
    mods = _mods(cstack, w_ada, b_ada).reshape(DEPTH, MOD_ROWS, 1, 3 * D_MODEL)

    glu_blk = COL_GLU // CAST_TN
    za_blk = COL_ZA // CAST_TN
    zb_blk = COL_ZB // CAST_TN
    n_glu = 2 * W_B // CAST_TN
    w_p = _cast_cols(w_in, N_QKVU, lambda j: jnp.where(j < n_glu, j + glu_blk, j - n_glu), "cast_w_qkvu")
    w_z = _cast_cols(w_in, N_ZG, lambda j: jnp.where(j == 0, za_blk, j + (zb_blk - 1)), "cast_w_zg")
    wpa, wpb, wo = _cast_many([w_proj_a, w_proj_b, w_out], "cast_w_merge")

    row = lambda a: a.reshape(DEPTH, 1, a.shape[-1])
    params = (row(rms_g), w_z, row(b_in), dw_w, row(dw_b), row(ln_g), row(ln_b), wpa, wpb, wo, final_g[None, :])
    past = cache_k.shape[2]
    ck = cache_k.reshape(dec_batch, DEPTH, past, W_A)
    cv = cache_v.reshape(dec_batch, DEPTH, past, W_A)
    table_flat = jnp.pad(rel_bias.reshape(-1, N_DC), ((0, 0), (0, LANES - N_DC)))

    x = x_prompt
    state = ()
    ctx_mod = lambda i: ctx_row
    for l in range(DEPTH):
        q, k, v, u, h, *state = _proj(x, mods, ctx_mod, l, params[0], w_p, params[2], tuple(state))
        x = _mix(x, mods, ctx_mod, l, params, q, k, v, u, h, None, None, final=(l == DEPTH - 1))
    bsz, seq, _ = x_prompt.shape
    state_k = state[0].reshape(bsz, DEPTH, seq, N_HEADS, HEAD_DIM)
    state_v = state[1].reshape(bsz, DEPTH, seq, N_HEADS, HEAD_DIM)

    z = x_sample
    lat_mod = lambda i: i
    for l in range(DEPTH):
        e_tiles = _bias_tiles(table_flat, l)
        q, k, v, u, h = _proj(z, mods, lat_mod, l, params[0], w_p, params[2], None)
        z = _mix(z, mods, lat_mod, l, params, q, k, v, u, h, (ck, cv), e_tiles, final=(l == DEPTH - 1))
    return (x, z, state_k, state_v)
```

```python
import functools

import jax
import jax.numpy as jnp
from jax import lax
from jax.experimental import pallas as pl
from jax.experimental.pallas import tpu as pltpu

F32 = jnp.float32
BF16 = jnp.bfloat16

D_MODEL = 1024
N_HEADS = 8
HEAD_DIM = 64
W_A = N_HEADS * HEAD_DIM
W_B = 512
CONV_K = 31
CONV_PAD = CONV_K // 2
GRID_W = 64
WIN_R = 8
WIN_C = 16
EPS = 1e-6
DEPTH = 2
N_DR = 2 * WIN_R - 1
N_DC = 2 * WIN_C - 1

COL_ZA = 3 * W_A
COL_GLU = COL_ZA + W_A
COL_ZB = COL_GLU + 2 * W_B
D_IN = COL_ZB + W_B + 2 * D_MODEL
N_QKVU = 3 * W_A + 2 * W_B
N_ZG = W_A + W_B + 2 * D_MODEL

LANES = 128
SUBLANES = 8
MXU_N = 256

ROWS_PER_TILE = 4
TILE = ROWS_PER_TILE * GRID_W
PROJ_TILE = 512
WIN_TILES = 3
N_WIN = WIN_TILES * TILE
HALO = 2 * SUBLANES
CONV_ROW_CHUNKS = 4
HEADS_PER_BLK = MXU_N // HEAD_DIM
N_BLK = N_HEADS // HEADS_PER_BLK
N_BIAS_V = ROWS_PER_TILE * WIN_TILES + ROWS_PER_TILE - 2
N_BAND_VARIANTS = 3
MOD_ROWS = 8
MOD_TN = 1024
CAST_TN = 512
VMEM_LIMIT = 58 * 1024 * 1024
NEG_INF = float("-inf")


def _rmsnorm(x, g):
    return x * lax.rsqrt(jnp.mean(x * x, axis=-1, keepdims=True) + EPS) * g


def _modulated(x, mod, g):
    shift = mod[:, :D_MODEL]
    scale = mod[:, D_MODEL:2 * D_MODEL]
    return _rmsnorm(x, g) * (1.0 + scale) + shift


def _sigmoid(x):
    return 0.5 * jnp.tanh(0.5 * x) + 0.5


def _silu(x):
    half = 0.5 * x
    return half * jnp.tanh(half) + half


def _dot_t(a, b):
    return lax.dot_general(a, b, (((1,), (1,)), ((), ())), preferred_element_type=F32)


def _mod_kernel(c_ref, w_ref, b_ref, o_ref):
    s = jax.nn.silu(c_ref[...])
    o_ref[0] = jnp.dot(s.astype(BF16), w_ref[0].astype(BF16), preferred_element_type=F32) + b_ref[0]


def _mods(cstack, w_ada, b_ada):
    n = 3 * D_MODEL
    return pl.pallas_call(
        _mod_kernel,
        grid=(DEPTH, n // MOD_TN),
        in_specs=[
            pl.BlockSpec((MOD_ROWS, D_MODEL), lambda l, j: (0, 0)),
            pl.BlockSpec((1, D_MODEL, MOD_TN), lambda l, j: (l, 0, j)),
            pl.BlockSpec((1, 1, MOD_TN), lambda l, j: (l, 0, j)),
        ],
        out_specs=pl.BlockSpec((1, MOD_ROWS, MOD_TN), lambda l, j: (l, 0, j)),
        out_shape=jax.ShapeDtypeStruct((DEPTH, MOD_ROWS, n), F32),
        name="adaln_mods",
    )(cstack, w_ada, b_ada.reshape(DEPTH, 1, n))


def _cast_kernel(w_ref, o_ref):
    o_ref[...] = w_ref[...].astype(BF16)


def _cast_cols(w, n_out, src_block, name):
    depth, kdim, _ = w.shape
    return pl.pallas_call(
        _cast_kernel,
        grid=(depth, n_out // CAST_TN),
        in_specs=[pl.BlockSpec((1, kdim, CAST_TN), lambda l, j: (l, 0, src_block(j)))],
        out_specs=pl.BlockSpec((1, kdim, CAST_TN), lambda l, j: (l, 0, j)),
        out_shape=jax.ShapeDtypeStruct((depth, kdim, n_out), BF16),
        name=name,
    )(w)


def _cast_many_kernel(*refs):
    n = len(refs) // 2
    for w_ref, o_ref in zip(refs[:n], refs[n:]):
        o_ref[...] = w_ref[...].astype(BF16)


def _cast_many(ws, name):
    depth, _, n = ws[0].shape
    spec = lambda w: pl.BlockSpec((1, w.shape[1], CAST_TN), lambda l, j: (l, 0, j))
    return pl.pallas_call(
        _cast_many_kernel,
        grid=(depth, n // CAST_TN),
        in_specs=[spec(w) for w in ws],
        out_specs=[spec(w) for w in ws],
        out_shape=[jax.ShapeDtypeStruct(w.shape, BF16) for w in ws],
        name=name,
    )(*ws)


def _bias_kernel(t_ref, e_ref, v_ref):
    layer = pl.program_id(0)
    h = pl.program_id(1)
    lane = lax.broadcasted_iota(jnp.int32, (GRID_W, LANES), 1)
    cq = lax.broadcasted_iota(jnp.int32, (GRID_W, LANES), 0)
    ck = lane & (GRID_W - 1)
    hi = lane >= GRID_W
    c0 = jnp.clip(cq - WIN_C // 2, 0, GRID_W - WIN_C)
    ok = (ck >= c0) & (ck < c0 + WIN_C)

    first = (layer * N_HEADS + h) * N_DR
    rows = [jnp.broadcast_to(t_ref[pl.ds(first + d, 1), :], (GRID_W, LANES)) for d in range(N_BIAS_V + 1)]

    def toeplitz(d, lane0):
        return pltpu.roll(rows[d], (lane0 - (WIN_C - 1)) % LANES, axis=1, stride=1, stride_axis=0)

    for d in range(N_BIAS_V):
        tile = jnp.where(hi, toeplitz(d + 1, GRID_W), toeplitz(d, 0))
        v_ref[d] = jnp.where(ok, tile, NEG_INF)
    neg = jnp.full((GRID_W, LANES), NEG_INF, F32)
    for var, shift in enumerate((1, 0, -1)):
        for i in range(WIN_TILES):
            cp = i + shift
            for rq in range(ROWS_PER_TILE):
                for p in range(MXU_N // LANES):
                    if 0 <= cp < WIN_TILES:
                        jr = ROWS_PER_TILE * cp + 2 * p
                        inside = [shift != 0 or 0 <= j - rq < WIN_R for j in (jr, jr + 1)]
                        tile = v_ref[jr - rq + (ROWS_PER_TILE - 1)]
                    else:
                        inside = [False, False]
                    if not any(inside):
                        tile = neg
                    elif not all(inside):
                        tile = jnp.where(hi, tile if inside[1] else neg, tile if inside[0] else neg)
                    e_ref[0, var, 0, i, rq * GRID_W:(rq + 1) * GRID_W, p * LANES:(p + 1) * LANES] = tile


def _bias_tiles(table_flat):
    shape = (DEPTH, N_BAND_VARIANTS, N_HEADS, WIN_TILES, TILE, MXU_N)
    return pl.pallas_call(
        _bias_kernel,
        grid=(DEPTH, N_HEADS),
        in_specs=[pl.BlockSpec(table_flat.shape, lambda l, h: (0, 0))],
        out_specs=pl.BlockSpec((1, N_BAND_VARIANTS, 1, WIN_TILES, TILE, MXU_N), lambda l, h: (l, 0, h, 0, 0, 0)),
        out_shape=jax.ShapeDtypeStruct(shape, F32),
        scratch_shapes=[pltpu.VMEM((N_BIAS_V, GRID_W, LANES), F32)],
        name="rel_bias_tiles",
    )(table_flat)


def _proj_kernel(*refs, n_state_in, all_layers):
    x_ref, mod_ref, g_ref, w_ref, b_ref = refs[:5]
    q_ref, k_ref, v_ref, u_ref, h_ref, *state_refs = refs[5 + n_state_in:]
    n_seq, tile, _ = x_ref.shape
    x = x_ref[...].reshape(n_seq * tile, D_MODEL)
    h = _modulated(x, mod_ref[0], g_ref[...]).astype(BF16)
    h_ref[...] = h.reshape(n_seq, tile, D_MODEL)
    bias = jnp.concatenate([b_ref[:, COL_GLU:COL_ZB], b_ref[:, :COL_ZA]], axis=1)
    p = jnp.dot(h, w_ref[...], preferred_element_type=F32) + bias
    a = p[:, :W_B]
    gate = p[:, W_B:2 * W_B]
    u_ref[...] = (a * jax.nn.sigmoid(gate)).reshape(n_seq, tile, W_B)
    q = (p[:, 2 * W_B:2 * W_B + W_A] * (HEAD_DIM ** -0.5)).astype(BF16)
    k = p[:, 2 * W_B + W_A:2 * W_B + 2 * W_A]
    v = p[:, 2 * W_B + 2 * W_A:]
    k16 = k.astype(BF16)
    v16 = v.astype(BF16)
    for s in range(n_seq):
        rows = slice(s * tile, (s + 1) * tile)
        for blk in range(N_BLK):
            cols = slice(blk * MXU_N, (blk + 1) * MXU_N)
            q_ref[s, blk] = q[rows, cols]
            k_ref[s, blk] = k16[rows, cols]
            v_ref[s, blk] = v16[rows, cols]
        if state_refs:
            for d in (range(DEPTH) if all_layers else range(1)):
                state_refs[0][s, d] = k[rows]
                state_refs[1][s, d] = v[rows]


def _layer_spec(shape, layer):
    return pl.BlockSpec((None,) + tuple(shape), lambda i, t: (layer,) + (0,) * len(shape))


def _weight_spec(shape, layer):
    return pl.BlockSpec((None,) + tuple(shape), lambda i, t: (layer,) + (0,) * len(shape),
                        pipeline_mode=pl.Buffered(1))


def _proj(x, mods, mod_row, layer, rms_g, w_p, b_in, state, shared_mod):
    bsz, seq, _ = x.shape
    tile = min(PROJ_TILE, seq)
    ns = PROJ_TILE // tile if shared_mod else 1
    tok = lambda i, t: (i, t, 0)
    blk_tok = lambda i, t: (i, 0, t, 0)
    out_shape = ([jax.ShapeDtypeStruct((bsz, N_BLK, seq, MXU_N), BF16)] * 3
                 + [jax.ShapeDtypeStruct((bsz, seq, W_B), F32), jax.ShapeDtypeStruct((bsz, seq, D_MODEL), BF16)])
    out_specs = ([pl.BlockSpec((ns, N_BLK, tile, MXU_N), blk_tok)] * 3
                 + [pl.BlockSpec((ns, tile, W_B), tok), pl.BlockSpec((ns, tile, D_MODEL), tok)])
    n_out = len(out_specs)
    in_specs = [
        pl.BlockSpec((ns, tile, D_MODEL), tok),
        pl.BlockSpec((None, 1, 1, 3 * D_MODEL), lambda i, t: (layer, mod_row(i), 0, 0)),
        _layer_spec((1, D_MODEL), layer),
        _weight_spec((D_MODEL, N_QKVU), layer),
        _layer_spec((1, D_IN), layer),
    ]
    args = [x, mods, rms_g, w_p, b_in]
    aliases = {}
    if state is not None:
        out_shape += [jax.ShapeDtypeStruct((bsz, DEPTH, seq, W_A), F32)] * 2
        if state:
            in_specs += [pl.BlockSpec(memory_space=pl.ANY)] * 2
            aliases = {len(args): n_out, len(args) + 1: n_out + 1}
            args += list(state)
            out_specs += [pl.BlockSpec((ns, 1, tile, W_A), lambda i, t: (i, layer, t, 0))] * 2
        else:
            out_specs += [pl.BlockSpec((ns, DEPTH, tile, W_A), lambda i, t: (i, 0, t, 0))] * 2
    return pl.pallas_call(
        functools.partial(_proj_kernel, n_state_in=len(state or ()), all_layers=(state == ())),
        grid=(bsz // ns, seq // tile),
        in_specs=in_specs,
        out_specs=out_specs,
        out_shape=out_shape,
        input_output_aliases=aliases,
        compiler_params=pltpu.CompilerParams(
            dimension_semantics=("arbitrary", "arbitrary"), vmem_limit_bytes=VMEM_LIMIT),
        name="proj_qkvu",
    )(*args)


def _softmax_pv(s_parts, v_parts):
    s = jnp.concatenate(s_parts, axis=1) if len(s_parts) > 1 else s_parts[0]
    m = jnp.max(s, axis=1, keepdims=True)
    p = jnp.exp(s - m)
    l = jnp.sum(p, axis=1, keepdims=True)
    p = p.astype(BF16)
    o = None
    start = 0
    for vp in v_parts:
        n = vp.shape[0]
        part = jnp.dot(p[:, start:start + n], vp, preferred_element_type=F32)
        o = part if o is None else o + part
        start += n
    return o / l


def _conv_ln_and_zg(u_ref, ubuf, dww_ref, dwb_ref, lng_ref, lnb_ref, h_ref, wz_ref, bias, t, n_tiles):
    if n_tiles == 1:
        ubuf[0:HALO, :] = jnp.zeros((HALO, W_B), F32)
        ubuf[HALO:HALO + TILE, :] = u_ref[0]
        ubuf[HALO + TILE:, :] = jnp.zeros((HALO, W_B), F32)
    else:
        t0 = pl.multiple_of(t * TILE, TILE)
        left = u_ref[0, pl.ds(pl.multiple_of(jnp.maximum(t0 - HALO, 0), HALO), HALO), :]
        right = u_ref[0, pl.ds(pl.multiple_of(jnp.minimum(t0 + TILE, n_tiles * TILE - HALO), HALO), HALO), :]
        ubuf[0:HALO, :] = jnp.where(t > 0, left, 0.0)
        ubuf[HALO:HALO + TILE, :] = u_ref[0, pl.ds(t0, TILE), :]
        ubuf[HALO + TILE:, :] = jnp.where(t < n_tiles - 1, right, 0.0)
    half = TILE // CONV_ROW_CHUNKS
    span = half + 4 * SUBLANES
    n_units = (W_B // LANES) * CONV_ROW_CHUNKS
    n_ztiles = N_ZG // MXU_N
    h16 = h_ref[0]
    units, ztiles = [], []

    def conv_unit(k):
        c, hf = divmod(k, CONV_ROW_CHUNKS)
        lanes = slice(c * LANES, (c + 1) * LANES)
        acc = jnp.zeros((half, LANES), F32)
        dep = (k * n_ztiles) // n_units - 3
        if dep >= 0:
            acc = acc + _held(ztiles[dep][0:half, 0:LANES])
        base = ubuf[hf * half:hf * half + span, lanes]
        for rr in range(SUBLANES):
            sh = base if rr == 0 else pltpu.roll(base, span - rr, axis=0)
            for j in range(4):
                kk = SUBLANES * j + rr - (HALO - CONV_PAD)
                if 0 <= kk < CONV_K:
                    acc = acc + dww_ref[kk:kk + 1, lanes] * sh[SUBLANES * j:SUBLANES * j + half]
        units.append(acc)

    def proj_tile(j):
        cols = slice(j * MXU_N, (j + 1) * MXU_N)
        lhs = h16
        dep = (j * n_units) // n_ztiles - 1
        if dep >= 0:
            z0 = _held(units[dep][0:2 * SUBLANES, :]).astype(BF16)
            top = h16[0:2 * SUBLANES] + jnp.concatenate([z0] * (D_MODEL // LANES), axis=1)
            lhs = jnp.concatenate([top, h16[2 * SUBLANES:]], axis=0)
        ztiles.append(jnp.dot(lhs, wz_ref[:, cols], preferred_element_type=F32) + bias[:, cols])

    events = sorted([(k * n_ztiles, 0, k) for k in range(n_units)] + [(j * n_units, 1, j) for j in range(n_ztiles)])
    for _, is_tile, idx in events:
        (proj_tile if is_tile else conv_unit)(idx)
    cv = jnp.concatenate(
        [jnp.concatenate(units[c * CONV_ROW_CHUNKS:(c + 1) * CONV_ROW_CHUNKS], axis=0) for c in range(W_B // LANES)],
        axis=1) + dwb_ref[...]
    mu = jnp.mean(cv, axis=-1, keepdims=True)
    var = jnp.mean(jnp.square(cv - mu), axis=-1, keepdims=True)
    ln = (cv - mu) * lax.rsqrt(var + EPS) * lng_ref[...] + lnb_ref[...]
    return ln, jnp.concatenate(ztiles, axis=1)


def _held(x):
    bits = pltpu.bitcast(x, jnp.uint32)
    bits = lax.shift_right_logical(lax.shift_right_logical(bits, jnp.uint32(16)), jnp.uint32(16))
    return pltpu.bitcast(bits, F32)


def _mix_kernel(*refs, latent, final, n_tiles):
    if latent:
        (x_ref, mod_ref, h_ref, wz_ref, b_ref, q_ref, k_ref, v_ref, u_ref, ck_ref, cv_ref, e_ref,
         dww_ref, dwb_ref, lng_ref, lnb_ref, wpa_ref, wpb_ref, wo_ref, fg_ref, y_ref, ubuf) = refs
    else:
        (x_ref, mod_ref, h_ref, wz_ref, b_ref, q_ref, k_ref, v_ref, u_ref,
         dww_ref, dwb_ref, lng_ref, lnb_ref, wpa_ref, wpb_ref, wo_ref, fg_ref, y_ref, ubuf) = refs
    t = pl.program_id(1)
    x = x_ref[0]
    mod = mod_ref[0]
    bias = jnp.concatenate([b_ref[:, COL_ZA:COL_GLU], b_ref[:, COL_ZB:]], axis=1)
    ln, zg = _conv_ln_and_zg(u_ref, ubuf, dww_ref, dwb_ref, lng_ref, lnb_ref, h_ref, wz_ref, bias, t, n_tiles)
    z_a = zg[:, :W_A]
    z_b = zg[:, W_A:W_A + W_B]
    g_a = _sigmoid(zg[:, W_A + W_B:W_A + W_B + D_MODEL])
    g_b = _sigmoid(zg[:, W_A + W_B + D_MODEL:])

    lane_row = lax.broadcasted_iota(jnp.int32, (1, MXU_N), 1)
    lane_full = lax.broadcasted_iota(jnp.int32, (TILE, MXU_N), 1)
    if latent:
        first_row = ROWS_PER_TILE * t
        win_row = jnp.clip(first_row - ROWS_PER_TILE, 0, (n_tiles - WIN_TILES) * ROWS_PER_TILE)
        win_tok = pl.multiple_of(win_row * GRID_W, TILE)
    blocks = []
    for blk in range(N_BLK):
        cols = slice(blk * MXU_N, (blk + 1) * MXU_N)
        qb = q_ref[0, blk]
        if latent:
            kw = k_ref[0, blk, pl.ds(win_tok, N_WIN), :]
            vw = v_ref[0, blk, pl.ds(win_tok, N_WIN), :]
            ckb = ck_ref[0, :, cols].astype(BF16)
            cvb = cv_ref[0, :, cols].astype(BF16)
        else:
            kw = k_ref[0, blk]
            vw = v_ref[0, blk]
        acc = jnp.zeros((TILE, MXU_N), F32)
        for hh in range(HEADS_PER_BLK):
            head = blk * HEADS_PER_BLK + hh
            lo = hh * HEAD_DIM
            hm = jnp.where((lane_row >= lo) & (lane_row < lo + HEAD_DIM), 1.0, 0.0).astype(BF16)
            qh = qb * hm
            s = _dot_t(qh, kw)
            if latent:
                parts = []
                for i in range(WIN_TILES):
                    parts.append(s[:, i * MXU_N:(i + 1) * MXU_N] + e_ref[head, i])
                parts.append(_dot_t(qh, ckb))
                o = _softmax_pv(parts, [vw, cvb])
            else:
                o = _softmax_pv([s], [vw])
            acc = jnp.where((lane_full >= lo) & (lane_full < lo + HEAD_DIM), o, acc)
        blocks.append(acc)
    attn = jnp.concatenate(blocks, axis=1)
    y_a = (attn * _silu(z_a)).astype(BF16)
    y_b = (_silu(ln) * _silu(z_b)).astype(BF16)

    m = (g_a * jnp.dot(y_a, wpa_ref[...], preferred_element_type=F32)
         + g_b * jnp.dot(y_b, wpb_ref[...], preferred_element_type=F32))
    o = jnp.dot(m.astype(BF16), wo_ref[...], preferred_element_type=F32)
    y = x + mod[:, 2 * D_MODEL:] * o
    if final:
        y = _rmsnorm(y, fg_ref[...])
    y_ref[0] = y


def _mix(x, mods, mod_row, layer, params, q, k, v, u, h, ctx_kv, e_tiles, final):
    _, w_z, b_in, dw_w, dw_b, ln_g, ln_b, wpa, wpb, wo, fg = params
    bsz, seq, _ = x.shape
    nt = seq // TILE
    latent = ctx_kv is not None
    tok = lambda i, t: (i, t, 0)
    in_specs = [
        pl.BlockSpec((1, TILE, D_MODEL), tok),
        pl.BlockSpec((None, 1, 1, 3 * D_MODEL), lambda i, t: (layer, mod_row(i), 0, 0)),
        pl.BlockSpec((1, TILE, D_MODEL), tok),
        _weight_spec((D_MODEL, N_ZG), layer),
        _layer_spec((1, D_IN), layer),
        pl.BlockSpec((1, N_BLK, TILE, MXU_N), lambda i, t: (i, 0, t, 0)),
        pl.BlockSpec((1, N_BLK, seq, MXU_N), lambda i, t: (i, 0, 0, 0)),
        pl.BlockSpec((1, N_BLK, seq, MXU_N), lambda i, t: (i, 0, 0, 0)),
        pl.BlockSpec((1, seq, W_B), lambda i, t: (i, 0, 0)),
    ]
    args = [x, mods, h, w_z, b_in, q, k, v, u]
    if latent:
        ck, cv = ctx_kv
        past = ck.shape[2]
        ctx_spec = pl.BlockSpec((1, None, past, W_A), lambda i, t: (i, layer, 0, 0))
        assert nt >= WIN_TILES
        variant = lambda t: jnp.where(t == 0, 0, jnp.where(t == nt - 1, N_BAND_VARIANTS - 1, 1))
        in_specs += [ctx_spec, ctx_spec,
                     pl.BlockSpec((None, None, N_HEADS, WIN_TILES, TILE, MXU_N),
                                  lambda i, t: (layer, variant(t), 0, 0, 0, 0))]
        args += [ck, cv, e_tiles]
    in_specs += [
        _layer_spec((CONV_K, W_B), layer),
        _layer_spec((1, W_B), layer),
        _layer_spec((1, W_B), layer),
        _layer_spec((1, W_B), layer),
        _weight_spec((W_A, D_MODEL), layer),
        _weight_spec((W_B, D_MODEL), layer),
        _weight_spec((D_MODEL, D_MODEL), layer),
        pl.BlockSpec((1, D_MODEL), lambda i, t: (0, 0)),
    ]
    args += [dw_w, dw_b, ln_g, ln_b, wpa, wpb, wo, fg]
    return pl.pallas_call(
        functools.partial(_mix_kernel, latent=latent, final=final, n_tiles=nt),
        grid=(bsz, nt),
        in_specs=in_specs,
        out_specs=pl.BlockSpec((1, TILE, D_MODEL), tok),
        out_shape=jax.ShapeDtypeStruct((bsz, seq, D_MODEL), F32),
        scratch_shapes=[pltpu.VMEM((TILE + 2 * HALO, W_B), F32)],
        compiler_params=pltpu.CompilerParams(
            dimension_semantics=("arbitrary", "arbitrary"), vmem_limit_bytes=VMEM_LIMIT),
        name="mix_latent" if latent else "mix_context",
    )(*args)


def kernel(x_prompt, x_sample, cache_k, cache_v, c, c_ctx, rms_g, w_ada, b_ada, w_in, b_in, rel_bias,
           dw_w, dw_b, ln_g, ln_b, w_proj_a, w_proj_b, w_out, final_g):
    dec_batch = x_sample.shape[0]
    ctx_row = dec_batch
    cstack = jnp.concatenate([c, c_ctx[None, :], jnp.zeros((MOD_ROWS - dec_batch - 1, D_MODEL), F32)], axis=0)
    mods = _mods(cstack, w_ada, b_ada).reshape(DEPTH, MOD_ROWS, 1, 3 * D_MODEL)

    glu_blk = COL_GLU // CAST_TN
    za_blk = COL_ZA // CAST_TN
    zb_blk = COL_ZB // CAST_TN
    n_glu = 2 * W_B // CAST_TN
    w_p = _cast_cols(w_in, N_QKVU, lambda j: jnp.where(j < n_glu, j + glu_blk, j - n_glu), "cast_w_qkvu")
    w_z = _cast_cols(w_in, N_ZG, lambda j: jnp.where(j == 0, za_blk, j + (zb_blk - 1)), "cast_w_zg")
    wpa, wpb, wo = _cast_many([w_proj_a, w_proj_b, w_out], "cast_w_merge")

    row = lambda a: a.reshape(DEPTH, 1, a.shape[-1])
    params = (row(rms_g), w_z, row(b_in), dw_w, row(dw_b), row(ln_g), row(ln_b), wpa, wpb, wo, final_g[None, :])
    past = cache_k.shape[2]
    ck = cache_k.reshape(dec_batch, DEPTH, past, W_A)
    cv = cache_v.reshape(dec_batch, DEPTH, past, W_A)
    table_flat = jnp.pad(rel_bias.reshape(-1, N_DC), ((0, 0), (0, LANES - N_DC)))

    x = x_prompt
    state = ()
    ctx_mod = lambda i: ctx_row
    for l in range(DEPTH):
        q, k, v, u, h, *state = _proj(x, mods, ctx_mod, l, params[0], w_p, params[2], tuple(state), True)
        x = _mix(x, mods, ctx_mod, l, params, q, k, v, u, h, None, None, final=(l == DEPTH - 1))
    bsz, seq, _ = x_prompt.shape
    state_k = state[0].reshape(bsz, DEPTH, seq, N_HEADS, HEAD_DIM)
    state_v = state[1].reshape(bsz, DEPTH, seq, N_HEADS, HEAD_DIM)

    z = x_sample
    lat_mod = lambda i: i
    e_tiles = _bias_tiles(table_flat)
    for l in range(DEPTH):
        q, k, v, u, h = _proj(z, mods, lat_mod, l, params[0], w_p, params[2], None, False)
        z = _mix(z, mods, lat_mod, l, params, q, k, v, u, h, (ck, cv), e_tiles, final=(l == DEPTH - 1))
    return (x, z, state_k, state_v)
```
